```python
import math
import jax, jax.numpy as jnp
from jax import lax
import numpy as np

D_MODEL = 1024
BATCH = 4
SEQ = 8192
DEPTH = 2

MIX_WIDTH = D_MODEL
DN_HEADS = 4
DN_HEAD_DIM = 128
DN_WIDTH = DN_HEADS * DN_HEAD_DIM
DN_CHUNK = 64
CONV_WIDTH = 3
DT_MIN = 1e-3
DT_MAX = 1e-1
SG_GROUPS = 4
SG_WIDTH = MIX_WIDTH - DN_WIDTH
SG_GROUP_DIM = SG_WIDTH // SG_GROUPS
SG_CHUNK = 128
FFN_HIDDEN = -(-8 * D_MODEL // (3 * 256)) * 256
EPS = 1e-6

OFF_Z = 3 * DN_WIDTH
OFF_A = 4 * DN_WIDTH
OFF_B = OFF_A + 2 * DN_HEADS
OFF_SG = OFF_B + 2 * DN_HEADS
PROJ_WIDTH = OFF_SG + 2 * SG_WIDTH

kernel_name = "hymba_gdn_gmlp_bidir_encoder"


def rms_norm(x, gain):
    xf = x.astype(jnp.float32)
    y = xf * lax.rsqrt(jnp.mean(xf * xf, axis=-1, keepdims=True) + EPS)
    return (y * gain.astype(jnp.float32)).astype(x.dtype)


def layer_norm(x, gain, bias):
    xf = x.astype(jnp.float32)
    mu = jnp.mean(xf, axis=-1, keepdims=True)
    xc = xf - mu
    y = xc * lax.rsqrt(jnp.mean(xc * xc, axis=-1, keepdims=True) + EPS)
    return (y * gain.astype(jnp.float32) + bias.astype(jnp.float32)).astype(x.dtype)


def l2_normalize(t):
    return t * lax.rsqrt(jnp.sum(t * t, axis=-1, keepdims=True) + EPS)


def centred_depthwise_conv(x, w):
    k_width, channels = w.shape
    pad = (k_width - 1) // 2
    return lax.conv_general_dilated(
        x, w[:, None, :], window_strides=(1,), padding=[(pad, pad)],
        dimension_numbers=("NWC", "WIO", "NWC"), feature_group_count=channels)


def gated_delta_rule(q, k, v, g, beta):
    bsz, heads, seq, dk = q.shape
    dv = v.shape[-1]
    n_chunks, c = seq // DN_CHUNK, DN_CHUNK
    q = q.reshape(bsz, heads, n_chunks, c, dk)
    k = k.reshape(bsz, heads, n_chunks, c, dk)
    v = v.reshape(bsz, heads, n_chunks, c, dv)
    g = jnp.cumsum(g.reshape(bsz, heads, n_chunks, c), axis=-1)
    beta = beta.reshape(bsz, heads, n_chunks, c)
    pos = jnp.arange(c)
    incl = pos[:, None] >= pos[None, :]
    strict = pos[:, None] > pos[None, :]
    decay = jnp.exp(jnp.where(incl, g[..., :, None] - g[..., None, :], -jnp.inf))
    k_beta = k * beta[..., None]
    lower = jnp.where(strict, jnp.einsum('bhnid,bhnjd->bhnij', k_beta, k) * decay, 0.0)
    rhs = jnp.concatenate([v * beta[..., None], k_beta * jnp.exp(g)[..., None]], axis=-1)
    sol = lax.linalg.triangular_solve(lower, rhs, left_side=True, lower=True, unit_diagonal=True)
    u_c, w_c = sol[..., :dv], sol[..., dv:]
    attn = jnp.einsum('bhnid,bhnjd->bhnij', q, k) * decay
    q_dec = q * jnp.exp(g)[..., None]
    g_last = g[..., -1]
    k_tail = k * jnp.exp(g_last[..., None] - g)[..., None]
    xs = (jnp.moveaxis(q_dec, 2, 0), jnp.moveaxis(k_tail, 2, 0), jnp.moveaxis(u_c, 2, 0),
          jnp.moveaxis(w_c, 2, 0), jnp.moveaxis(attn, 2, 0), jnp.moveaxis(g_last, 2, 0))

    def step(state, inp):
        q_i, k_i, u_i, w_i, a_i, gl_i = inp
        v_new = u_i - jnp.einsum('bhcd,bhde->bhce', w_i, state)
        o_i = jnp.einsum('bhcd,bhde->bhce', q_i, state) + jnp.einsum('bhij,bhje->bhie', a_i, v_new)
        state = state * jnp.exp(gl_i)[..., None, None] + jnp.einsum('bhcd,bhce->bhde', k_i, v_new)
        return state, o_i

    s0 = jnp.zeros((bsz, heads, dk, dv), jnp.float32)
    _, o = lax.scan(step, s0, xs)
    return jnp.moveaxis(o, 0, 2).reshape(bsz, heads, seq, dv)


def deltanet_group(p, conv_w, a_log, dt_bias, norm_g):
    bsz, seq, _ = p.shape
    qkv = jax.nn.silu(centred_depthwise_conv(p[..., :OFF_Z], conv_w))
    z = p[..., OFF_Z:OFF_A]
    a = p[..., OFF_A:OFF_B].reshape(bsz, seq, 2, DN_HEADS).astype(jnp.float32)
    b = p[..., OFF_B:OFF_SG].reshape(bsz, seq, 2, DN_HEADS).astype(jnp.float32)

    def heads(t):
        return t.reshape(bsz, seq, DN_HEADS, DN_HEAD_DIM).transpose(0, 2, 1, 3).astype(jnp.float32)

    q = l2_normalize(heads(qkv[..., :DN_WIDTH])) * (DN_HEAD_DIM ** -0.5)
    k = l2_normalize(heads(qkv[..., DN_WIDTH:2 * DN_WIDTH]))
    v = heads(qkv[..., 2 * DN_WIDTH:])
    g = -jnp.exp(a_log.astype(jnp.float32)) * jax.nn.softplus(a + dt_bias.astype(jnp.float32))
    g = jnp.transpose(g, (2, 0, 3, 1))
    beta = jnp.transpose(jax.nn.sigmoid(b), (2, 0, 3, 1))
    o_fwd = gated_delta_rule(q, k, v, g[0], beta[0])
    o_bwd = jnp.flip(gated_delta_rule(jnp.flip(q, 2), jnp.flip(k, 2), jnp.flip(v, 2),
                                      jnp.flip(g[1], 2), jnp.flip(beta[1], 2)), 2)
    o = (o_fwd + o_bwd).transpose(0, 2, 1, 3)
    zf = z.reshape(bsz, seq, DN_HEADS, DN_HEAD_DIM).astype(jnp.float32)
    o = rms_norm(o, norm_g) * jax.nn.silu(zf)
    return o.reshape(bsz, seq, DN_WIDTH)


def spatial_gating_group(p, ln_g, ln_b, w_s, b_s, out_g):
    bsz, seq, _ = p.shape
    p = jax.nn.gelu(p)
    u, v = p[..., :SG_WIDTH], p[..., SG_WIDTH:]
    v = layer_norm(v, ln_g, ln_b)
    shape5 = (bsz, seq // SG_CHUNK, SG_CHUNK, SG_GROUPS, SG_GROUP_DIM)
    v = v.reshape(shape5)
    mixed = jnp.einsum('gij,bnjgc->bnigc', w_s, v) + b_s.T[:, :, None]
    y = u.reshape(shape5) * mixed
    y = rms_norm(y, out_g.reshape(SG_GROUPS, SG_GROUP_DIM))
    return y.reshape(bsz, seq, SG_WIDTH)


def setup_inputs(seed: int = 0) -> dict:
    key = jax.random.key(seed)
    ks = jax.random.split(key, 18)
    f32 = jnp.float32

    def normal(k, shape, scale):
        return jax.random.normal(k, shape, f32) * scale

    def gain(k, shape):
        return 1.0 + 0.05 * jax.random.normal(k, shape, f32)

    x = jax.random.normal(ks[0], (BATCH, SEQ, D_MODEL), f32)
    mix_norm_g = gain(ks[1], (DEPTH, D_MODEL))
    w_in = normal(ks[2], (DEPTH, D_MODEL, PROJ_WIDTH), D_MODEL ** -0.5)
    conv_w = normal(ks[3], (DEPTH, CONV_WIDTH, 3 * DN_WIDTH), CONV_WIDTH ** -0.5)
    dn_a_log = jnp.log(jax.random.uniform(ks[4], (DEPTH, 2, DN_HEADS), f32, 1.0, 16.0))
    dt = jnp.exp(jax.random.uniform(ks[5], (DEPTH, 2, DN_HEADS), f32,
                                    math.log(DT_MIN), math.log(DT_MAX)))
    dn_dt_bias = dt + jnp.log(-jnp.expm1(-dt))
    dn_norm_g = gain(ks[6], (DEPTH, DN_HEAD_DIM))
    sg_ln_g = gain(ks[7], (DEPTH, SG_WIDTH))
    sg_ln_b = normal(ks[8], (DEPTH, SG_WIDTH), 0.02)
    sg_w = normal(ks[9], (DEPTH, SG_GROUPS, SG_CHUNK, SG_CHUNK), SG_CHUNK ** -0.5)
    sg_b = gain(ks[10], (DEPTH, SG_GROUPS, SG_CHUNK))
    sg_out_g = gain(ks[11], (DEPTH, SG_WIDTH))
    w_out = normal(ks[12], (DEPTH, MIX_WIDTH, D_MODEL), MIX_WIDTH ** -0.5)
    ffn_norm_g = gain(ks[13], (DEPTH, D_MODEL))
    w_gate = normal(ks[14], (DEPTH, D_MODEL, FFN_HIDDEN), D_MODEL ** -0.5)
    w_up = normal(ks[15], (DEPTH, D_MODEL, FFN_HIDDEN), D_MODEL ** -0.5)
    w_down = normal(ks[16], (DEPTH, FFN_HIDDEN, D_MODEL), FFN_HIDDEN ** -0.5)
    final_norm_g = gain(ks[17], (D_MODEL,))
    return {"x": x, "mix_norm_g": mix_norm_g, "w_in": w_in, "conv_w": conv_w,
            "dn_a_log": dn_a_log, "dn_dt_bias": dn_dt_bias, "dn_norm_g": dn_norm_g,
            "sg_ln_g": sg_ln_g, "sg_ln_b": sg_ln_b, "sg_w": sg_w, "sg_b": sg_b,
            "sg_out_g": sg_out_g, "w_out": w_out, "ffn_norm_g": ffn_norm_g,
            "w_gate": w_gate, "w_up": w_up, "w_down": w_down, "final_norm_g": final_norm_g}


def reference(x, mix_norm_g, w_in, conv_w, dn_a_log, dn_dt_bias, dn_norm_g, sg_ln_g, sg_ln_b,
              sg_w, sg_b, sg_out_g, w_out, ffn_norm_g, w_gate, w_up, w_down, final_norm_g):
    for l in range(DEPTH):
        h = rms_norm(x, mix_norm_g[l])
        proj = jnp.einsum('bsd,dp->bsp', h, w_in[l])
        y_a = deltanet_group(proj[..., :OFF_SG], conv_w[l], dn_a_log[l], dn_dt_bias[l], dn_norm_g[l])
        y_b = spatial_gating_group(proj[..., OFF_SG:], sg_ln_g[l], sg_ln_b[l], sg_w[l], sg_b[l],
                                   sg_out_g[l])
        mix = jnp.concatenate([y_a.astype(x.dtype), y_b.astype(x.dtype)], axis=-1)
        x = x + jnp.einsum('bsm,md->bsd', mix, w_out[l])
        h = rms_norm(x, ffn_norm_g[l])
        hid = jax.nn.silu(jnp.einsum('bsd,df->bsf', h, w_gate[l])) * jnp.einsum('bsd,df->bsf', h, w_up[l])
        x = x + jnp.einsum('bsf,fd->bsd', hid, w_down[l])
    return rms_norm(x, final_norm_g)
```

```python
import functools

import jax
import jax.numpy as jnp
from jax import lax
from jax.experimental import pallas as pl
from jax.experimental.pallas import tpu as pltpu

F32 = jnp.float32
BF16 = jnp.bfloat16

DN_HEADS = 4
DN_HEAD_DIM = 128
DN_WIDTH = DN_HEADS * DN_HEAD_DIM
SG_GROUPS = 4
SG_GROUP_DIM = 128
SG_WIDTH = SG_GROUPS * SG_GROUP_DIM
SG_CHUNK = 128
EPS = 1e-6

OFF_Z = 3 * DN_WIDTH
OFF_A = 4 * DN_WIDTH
OFF_B = OFF_A + 2 * DN_HEADS
OFF_SG = OFF_B + 2 * DN_HEADS

LANES = 128
BF16_SUBLANES = 16
VMEM_LIMIT_BYTES = 56 * 1024 * 1024

DN_CHUNK = 128
TOKEN_TILE = 512


def _rms(x, gain):
    return x * lax.rsqrt(jnp.mean(x * x, axis=-1, keepdims=True) + EPS) * gain


def _silu(x):
    return x * jax.nn.sigmoid(x)


def _gelu_tanh(x):
    c = 0.7978845608028654
    return x * (0.5 * (1.0 + jnp.tanh(c * (x + 0.044715 * (x * x * x)))))


def _bdot(a, b):
    return jnp.dot(a.astype(BF16), b.astype(BF16), preferred_element_type=F32)


def _inproj_kernel(x_ref, ng_ref, wqkvz_ref, wab_ref, wsg_ref, lng_ref, lnb_ref, ws_ref, bst_ref,
                   og_ref, qkv_ref, z_ref, gates_ref, yb_ref):
    tm = x_ref.shape[0]
    hb = _rms(x_ref[...], ng_ref[...]).astype(BF16)
    pq = jnp.dot(hb, wqkvz_ref[...], preferred_element_type=F32)
    qkv_ref[...] = pq[:, :OFF_Z].astype(BF16)
    z_ref[...] = pq[:, OFF_Z:].astype(BF16)
    gates_ref[...] = jnp.dot(hb, wab_ref[...], preferred_element_type=F32)

    ps = _gelu_tanh(jnp.dot(hb, wsg_ref[...], preferred_element_type=F32))
    u = ps[:, :SG_WIDTH]
    v = ps[:, SG_WIDTH:]
    vc = v - jnp.mean(v, axis=-1, keepdims=True)
    v = vc * lax.rsqrt(jnp.mean(vc * vc, axis=-1, keepdims=True) + EPS) * lng_ref[...] + lnb_ref[...]
    vb = v.astype(BF16)
    for n in range(tm // SG_CHUNK):
        rows = slice(n * SG_CHUNK, (n + 1) * SG_CHUNK)
        for g in range(SG_GROUPS):
            cols = slice(g * SG_GROUP_DIM, (g + 1) * SG_GROUP_DIM)
            mixed = jnp.dot(ws_ref[g], vb[rows, cols], preferred_element_type=F32) + bst_ref[:, g:g + 1]
            y = _rms(u[rows, cols] * mixed, og_ref[:, cols])
            yb_ref[rows, cols] = y.astype(BF16)


def _inproj(x2, ng, wqkvz, wab, wsg, lng, lnb, ws, bst, og):
    t, d = x2.shape
    tm = TOKEN_TILE
    full = lambda *shape: pl.BlockSpec(shape, lambda i: (0,) * len(shape))
    rows = lambda w: pl.BlockSpec((tm, w), lambda i: (i, 0))
    return pl.pallas_call(
        _inproj_kernel,
        grid=(t // tm,),
        in_specs=[rows(d), full(1, d), full(d, 4 * DN_WIDTH), full(d, LANES), full(d, 2 * SG_WIDTH),
                  full(1, SG_WIDTH), full(1, SG_WIDTH), full(SG_GROUPS, SG_CHUNK, SG_CHUNK),
                  full(SG_CHUNK, SG_GROUPS), full(1, SG_WIDTH)],
        out_specs=[rows(OFF_Z), rows(DN_WIDTH), rows(LANES), rows(SG_WIDTH)],
        out_shape=[jax.ShapeDtypeStruct((t, OFF_Z), BF16), jax.ShapeDtypeStruct((t, DN_WIDTH), BF16),
                   jax.ShapeDtypeStruct((t, LANES), F32), jax.ShapeDtypeStruct((t, SG_WIDTH), BF16)],
        compiler_params=pltpu.CompilerParams(dimension_semantics=("arbitrary",),
                                             vmem_limit_bytes=VMEM_LIMIT_BYTES),
        name="inproj_sg",
    )(x2, ng, wqkvz, wab, wsg, lng, lnb, ws, bst, og)


def _pair(a, b):
    return jnp.concatenate([a, b], axis=1)


def _blockdiag(p):
    n = p.shape[0]
    zero = jnp.zeros((n, n), p.dtype)
    return jnp.concatenate([jnp.concatenate([p[:, :n], zero], axis=1),
                            jnp.concatenate([zero, p[:, n:]], axis=1)], axis=0)


def _pdot(p, q):
    return jnp.dot(p.astype(BF16), _blockdiag(q.astype(BF16)), preferred_element_type=F32)


def _unit_tri_inverse(a2, row, colm):
    eye = (row == colm).astype(F32)
    base = 8
    ld = jnp.where((row // base) == (colm // base), a2, 0.0)
    ld2 = _pdot(ld, ld)
    ld4 = _pdot(ld2, ld2)
    t = eye - ld
    t = t + _pdot(t, ld2)
    t = t + _pdot(t, ld4)
    m = base
    while m < DN_CHUNK:
        joins = ((row // (2 * m)) == (colm // (2 * m))) & ((row // m) != (colm // m))
        t = t - _pdot(t, _pdot(jnp.where(joins, a2, 0.0), t))
        m *= 2
    return t


def _chunk_cumsum(x, rows, reverse):
    n = x.shape[0]
    s = 1
    while s < n:
        if reverse:
            x = x + jnp.where(rows < n - s, pltpu.roll(x, n - s, 0), 0.0)
        else:
            x = x + jnp.where(rows >= s, pltpu.roll(x, s, 0), 0.0)
        s *= 2
    return x


def _delta_direction(reverse, chunk, n_chunks, x_ref, xp_ref, xn_ref, g_ref, cw_ref, alog_ref, dtb_ref,
                     o_ref, s_ref):
    c = DN_CHUNK
    row = lax.broadcasted_iota(jnp.int32, (c, c), 0)
    col = lax.broadcasted_iota(jnp.int32, (c, c), 1)
    row2 = lax.broadcasted_iota(jnp.int32, (c, 2 * c), 0)
    colm2 = lax.broadcasted_iota(jnp.int32, (c, 2 * c), 1) % c

    x = x_ref[...].astype(F32)
    rows_x = lax.broadcasted_iota(jnp.int32, x.shape, 0)
    has_prev = (chunk > 0).astype(F32)
    has_next = (chunk < n_chunks - 1).astype(F32)
    prev_row = xp_ref[BF16_SUBLANES - 1:BF16_SUBLANES, :].astype(F32) * has_prev
    next_row = xn_ref[0:1, :].astype(F32) * has_next
    x_prev = jnp.where(rows_x == 0, prev_row, pltpu.roll(x, 1, 0))
    x_next = jnp.where(rows_x == c - 1, next_row, pltpu.roll(x, c - 1, 0))
    qkv = _silu(cw_ref[0:1, :] * x_prev + cw_ref[1:2, :] * x + cw_ref[2:3, :] * x_next)

    gates = g_ref[...]
    z = gates + dtb_ref[...]
    softplus = jnp.maximum(z, 0.0) + jnp.log(1.0 + jnp.exp(-jnp.abs(z)))
    g = -jnp.exp(alog_ref[...]) * softplus
    beta = jax.nn.sigmoid(gates)
    gc = _chunk_cumsum(g, row, reverse)
    gct = gc.T
    g_last = gc[0:1, :] if reverse else gc[c - 1:c, :]
    g_last_t = gct[:, 0:1] if reverse else gct[:, c - 1:c]
    e_gc = jnp.exp(gc)
    e_tail_t = jnp.exp(g_last_t - gct)
    e_last = jnp.exp(g_last)

    incl = (row <= col) if reverse else (row >= col)
    strict = (row < col) if reverse else (row > col)
    d = 1 if reverse else 0

    for pair_idx in range(DN_HEADS // 2):
        qs, kts, ktails, rhss, decays, e_lasts = [], [], [], [], [], []
        for h in (2 * pair_idx, 2 * pair_idx + 1):
            r = d * DN_HEADS + h
            q = qkv[:, h * DN_HEAD_DIM:(h + 1) * DN_HEAD_DIM]
            k = qkv[:, DN_WIDTH + h * DN_HEAD_DIM:DN_WIDTH + (h + 1) * DN_HEAD_DIM]
            v = qkv[:, 2 * DN_WIDTH + h * DN_HEAD_DIM:2 * DN_WIDTH + (h + 1) * DN_HEAD_DIM]
            q = q * (lax.rsqrt(jnp.sum(q * q, axis=-1, keepdims=True) + EPS) * (DN_HEAD_DIM ** -0.5))
            k = k * lax.rsqrt(jnp.sum(k * k, axis=-1, keepdims=True) + EPS)
            b_col = beta[:, 2 * DN_HEADS + r:2 * DN_HEADS + r + 1]
            g_col = gc[:, r:r + 1]
            g_row = gct[r:r + 1, :]
            eg_col = e_gc[:, r:r + 1]
            kt = k.T
            kb = k * b_col
            decays.append(jnp.exp(jnp.where(incl, g_col - g_row, -1e30)))
            qs.append((kb, q, q * eg_col))
            kts.append(kt)
            ktails.append(kt * e_tail_t[r:r + 1, :])
            rhss.append(jnp.concatenate([v * b_col, kb * eg_col], axis=1))
            e_lasts.append(jnp.broadcast_to(e_last[:, r:r + 1], (1, DN_HEAD_DIM)))

        lhs = jnp.concatenate([_pair(qs[0][0], qs[1][0]), _pair(qs[0][1], qs[1][1])], axis=0)
        gram = _pdot(lhs, _pair(kts[0], kts[1]))
        decay2 = _pair(decays[0], decays[1])
        strict2 = _pair(strict, strict)
        a2 = jnp.where(strict2, gram[:c] * decay2, 0.0)
        attn2 = gram[c:] * decay2
        t2 = _unit_tri_inverse(a2, row2, colm2)
        u_list, w_list = [], []
        for i in range(2):
            uw = _bdot(t2[:, i * c:(i + 1) * c], rhss[i])
            u_list.append(uw[:, :DN_HEAD_DIM])
            w_list.append(uw[:, DN_HEAD_DIM:])

        s2 = s_ref[d * (DN_HEADS // 2) + pair_idx]
        lhs = jnp.concatenate([_pair(w_list[0], w_list[1]), _pair(qs[0][2], qs[1][2])], axis=0)
        rs = _pdot(lhs, s2)
        v_new = _pair(u_list[0], u_list[1]) - rs[:c]
        lhs = jnp.concatenate([attn2, _pair(ktails[0], ktails[1])], axis=0)
        rv = _pdot(lhs, v_new)
        o_ref[:, 2 * pair_idx * DN_HEAD_DIM:(2 * pair_idx + 2) * DN_HEAD_DIM] = rs[c:] + rv[:c]
        s_ref[d * (DN_HEADS // 2) + pair_idx] = s2 * _pair(e_lasts[0], e_lasts[1]) + rv[c:]


def _delta_kernel(xf_ref, xfp_ref, xfn_ref, xb_ref, xbp_ref, xbn_ref, gf_ref, gb_ref, cw_ref, alog_ref,
                  dtb_ref, of_ref, ob_ref, s_ref):
    n = pl.program_id(1)
    n_chunks = pl.num_programs(1)

    @pl.when(n == 0)
    def _():
        s_ref[...] = jnp.zeros_like(s_ref)

    _delta_direction(False, n, n_chunks, xf_ref, xfp_ref, xfn_ref, gf_ref, cw_ref, alog_ref, dtb_ref,
                     of_ref, s_ref)
    _delta_direction(True, n_chunks - 1 - n, n_chunks, xb_ref, xbp_ref, xbn_ref, gb_ref, cw_ref, alog_ref,
                     dtb_ref, ob_ref, s_ref)


def _delta_rule(qkv, gates, conv_w, alog_row, dtb_row):
    bsz, seq, width = qkv.shape
    c = DN_CHUNK
    nc = seq // c
    halo = BF16_SUBLANES
    per_chunk = c // halo
    n_halo = seq // halo

    fwd = lambda n: n
    bwd = lambda n: nc - 1 - n

    def specs(chunk_of):
        return [
            pl.BlockSpec((None, c, width), lambda b, n: (b, chunk_of(n), 0)),
            pl.BlockSpec((None, halo, width),
                         lambda b, n: (b, jnp.maximum(chunk_of(n) * per_chunk - 1, 0), 0)),
            pl.BlockSpec((None, halo, width),
                         lambda b, n: (b, jnp.minimum((chunk_of(n) + 1) * per_chunk, n_halo - 1), 0)),
        ]

    gate_spec = lambda chunk_of: pl.BlockSpec((None, c, LANES), lambda b, n: (b, chunk_of(n), 0))
    out_spec = lambda chunk_of: pl.BlockSpec((None, c, DN_WIDTH), lambda b, n: (b, chunk_of(n), 0))
    full = lambda *shape: pl.BlockSpec(shape, lambda b, n: (0,) * len(shape))
    out_sds = jax.ShapeDtypeStruct((bsz, seq, DN_WIDTH), F32)
    return pl.pallas_call(
        _delta_kernel,
        grid=(bsz, nc),
        in_specs=specs(fwd) + specs(bwd) + [gate_spec(fwd), gate_spec(bwd), full(3, width),
                                            full(1, LANES), full(1, LANES)],
        out_specs=[out_spec(fwd), out_spec(bwd)],
        out_shape=[out_sds, out_sds],
        scratch_shapes=[pltpu.VMEM((2 * (DN_HEADS // 2), DN_HEAD_DIM, 2 * DN_HEAD_DIM), F32)],
        compiler_params=pltpu.CompilerParams(dimension_semantics=("arbitrary", "arbitrary"),
                                             vmem_limit_bytes=VMEM_LIMIT_BYTES),
        name="delta_rule",
    )(qkv, qkv, qkv, qkv, qkv, qkv, gates, gates, conv_w, alog_row, dtb_row)


def _mix_ffn_kernel(x_ref, of_ref, ob_ref, z_ref, yb_ref, dng_ref, wo_ref, fg_ref, wg_ref, wu_ref, wd_ref,
                    fin_ref, out_ref, *, final):
    o = of_ref[...] + ob_ref[...]
    z = z_ref[...].astype(F32)
    heads = []
    for h in range(DN_HEADS):
        cols = slice(h * DN_HEAD_DIM, (h + 1) * DN_HEAD_DIM)
        heads.append((_rms(o[:, cols], dng_ref[...]) * _silu(z[:, cols])).astype(BF16))
    ya = jnp.concatenate(heads, axis=1)
    x1 = (x_ref[...] + jnp.dot(ya, wo_ref[:DN_WIDTH, :], preferred_element_type=F32)
          + jnp.dot(yb_ref[...], wo_ref[DN_WIDTH:, :], preferred_element_type=F32))
    hb = _rms(x1, fg_ref[...]).astype(BF16)
    gate = jnp.dot(hb, wg_ref[...], preferred_element_type=F32)
    up = jnp.dot(hb, wu_ref[...], preferred_element_type=F32)
    hid = (_silu(gate) * up).astype(BF16)
    out = x1 + jnp.dot(hid, wd_ref[...], preferred_element_type=F32)
    if final:
        out = _rms(out, fin_ref[...])
    out_ref[...] = out


def _mix_ffn(x2, o_f, o_b, z, yb, dng, wo, fg, wg, wu, wd, fin, final):
    t, d = x2.shape
    f = wg.shape[1]
    tm = TOKEN_TILE
    once = pl.Buffered(1)
    full = lambda *shape: pl.BlockSpec(shape, lambda i: (0,) * len(shape), pipeline_mode=once)
    rows = lambda w: pl.BlockSpec((tm, w), lambda i: (i, 0))
    return pl.pallas_call(
        functools.partial(_mix_ffn_kernel, final=final),
        grid=(t // tm,),
        in_specs=[rows(d), rows(DN_WIDTH), rows(DN_WIDTH), rows(DN_WIDTH), rows(SG_WIDTH),
                  full(1, DN_HEAD_DIM), full(DN_WIDTH + SG_WIDTH, d), full(1, d), full(d, f), full(d, f),
                  full(f, d), full(1, d)],
        out_specs=rows(d),
        out_shape=jax.ShapeDtypeStruct((t, d), F32),
        compiler_params=pltpu.CompilerParams(dimension_semantics=("arbitrary",),
                                             vmem_limit_bytes=VMEM_LIMIT_BYTES),
        name="mix_ffn",
    )(x2, o_f, o_b, z, yb, dng, wo, fg, wg, wu, wd, fin)


def _pad_lanes(v):
    flat = v.reshape(1, -1).astype(F32)
    return jnp.pad(flat, ((0, 0), (0, LANES - flat.shape[1])))


def kernel(x, mix_norm_g, w_in, conv_w, dn_a_log, dn_dt_bias, dn_norm_g, sg_ln_g, sg_ln_b, sg_w, sg_b,
           sg_out_g, w_out, ffn_norm_g, w_gate, w_up, w_down, final_norm_g):
    bsz, seq, d = x.shape
    depth = w_in.shape[0]
    assert seq % TOKEN_TILE == 0 and seq % DN_CHUNK == 0 and TOKEN_TILE % SG_CHUNK == 0
    x2 = x.reshape(bsz * seq, d)
    row = lambda v: v.reshape(1, -1).astype(F32)
    for l in range(depth):
        wl = w_in[l]
        wqkvz = wl[:, :OFF_A].astype(BF16)
        wab = jnp.pad(wl[:, OFF_A:OFF_SG], ((0, 0), (0, LANES - (OFF_SG - OFF_A)))).astype(BF16)
        wsg = wl[:, OFF_SG:].astype(BF16)
        qkv, z, gates, yb = _inproj(x2, row(mix_norm_g[l]), wqkvz, wab, wsg, row(sg_ln_g[l]),
                                    row(sg_ln_b[l]), sg_w[l].astype(BF16), sg_b[l].T.astype(F32),
                                    row(sg_out_g[l]))
        o_f, o_b = _delta_rule(qkv.reshape(bsz, seq, OFF_Z), gates.reshape(bsz, seq, LANES),
                               conv_w[l].astype(F32), _pad_lanes(dn_a_log[l]), _pad_lanes(dn_dt_bias[l]))
        x2 = _mix_ffn(x2, o_f.reshape(bsz * seq, DN_WIDTH), o_b.reshape(bsz * seq, DN_WIDTH), z, yb,
                      row(dn_norm_g[l]), w_out[l].astype(BF16), row(ffn_norm_g[l]),
                      w_gate[l].astype(BF16), w_up[l].astype(BF16), w_down[l].astype(BF16),
                      row(final_norm_g), final=(l == depth - 1))
    return x2.reshape(bsz, seq, d)
```

```python
import functools

import jax
import jax.numpy as jnp
from jax import lax
from jax.experimental import pallas as pl
from jax.experimental.pallas import tpu as pltpu

F32 = jnp.float32
BF16 = jnp.bfloat16

DN_HEADS = 4
DN_HEAD_DIM = 128
DN_WIDTH = DN_HEADS * DN_HEAD_DIM
SG_GROUPS = 4
SG_GROUP_DIM = 128
SG_WIDTH = SG_GROUPS * SG_GROUP_DIM
SG_CHUNK = 128
EPS = 1e-6

OFF_Z = 3 * DN_WIDTH
OFF_A = 4 * DN_WIDTH
OFF_B = OFF_A + 2 * DN_HEADS
OFF_SG = OFF_B + 2 * DN_HEADS

LANES = 128
BF16_SUBLANES = 16
VMEM_LIMIT_BYTES = 56 * 1024 * 1024

DN_CHUNK = 128
DN_BATCH_ROWS = 2
TOKEN_TILE = 512


def _rms(x, gain):
    return x * lax.rsqrt(jnp.mean(x * x, axis=-1, keepdims=True) + EPS) * gain


def _silu(x):
    return x * jax.nn.sigmoid(x)


def _gelu_tanh(x):
    c = 0.7978845608028654
    return x * (0.5 * (1.0 + jnp.tanh(c * (x + 0.044715 * (x * x * x)))))


def _bdot(a, b):
    return jnp.dot(a.astype(BF16), b.astype(BF16), preferred_element_type=F32)


def _inproj_kernel(x_ref, ng_ref, wqkvz_ref, wab_ref, wsg_ref, lng_ref, lnb_ref, ws_ref, bst_ref,
                   og_ref, qkv_ref, z_ref, gates_ref, yb_ref):
    tm = x_ref.shape[0]
    hb = _rms(x_ref[...], ng_ref[...]).astype(BF16)
    pq = jnp.dot(hb, wqkvz_ref[...], preferred_element_type=F32)
    qkv_ref[...] = pq[:, :OFF_Z].astype(BF16)
    z_ref[...] = pq[:, OFF_Z:].astype(BF16)
    gates_ref[...] = jnp.dot(hb, wab_ref[...], preferred_element_type=F32)

    ps = _gelu_tanh(jnp.dot(hb, wsg_ref[...], preferred_element_type=F32))
    u = ps[:, :SG_WIDTH]
    v = ps[:, SG_WIDTH:]
    vc = v - jnp.mean(v, axis=-1, keepdims=True)
    v = vc * lax.rsqrt(jnp.mean(vc * vc, axis=-1, keepdims=True) + EPS) * lng_ref[...] + lnb_ref[...]
    vb = v.astype(BF16)
    for n in range(tm // SG_CHUNK):
        rows = slice(n * SG_CHUNK, (n + 1) * SG_CHUNK)
        for g in range(SG_GROUPS):
            cols = slice(g * SG_GROUP_DIM, (g + 1) * SG_GROUP_DIM)
            mixed = jnp.dot(ws_ref[g], vb[rows, cols], preferred_element_type=F32) + bst_ref[:, g:g + 1]
            y = _rms(u[rows, cols] * mixed, og_ref[:, cols])
            yb_ref[rows, cols] = y.astype(BF16)


def _inproj(x2, ng, wqkvz, wab, wsg, lng, lnb, ws, bst, og):
    t, d = x2.shape
    tm = TOKEN_TILE
    full = lambda *shape: pl.BlockSpec(shape, lambda i: (0,) * len(shape))
    rows = lambda w: pl.BlockSpec((tm, w), lambda i: (i, 0))
    return pl.pallas_call(
        _inproj_kernel,
        grid=(t // tm,),
        in_specs=[rows(d), full(1, d), full(d, 4 * DN_WIDTH), full(d, LANES), full(d, 2 * SG_WIDTH),
                  full(1, SG_WIDTH), full(1, SG_WIDTH), full(SG_GROUPS, SG_CHUNK, SG_CHUNK),
                  full(SG_CHUNK, SG_GROUPS), full(1, SG_WIDTH)],
        out_specs=[rows(OFF_Z), rows(DN_WIDTH), rows(LANES), rows(SG_WIDTH)],
        out_shape=[jax.ShapeDtypeStruct((t, OFF_Z), BF16), jax.ShapeDtypeStruct((t, DN_WIDTH), BF16),
                   jax.ShapeDtypeStruct((t, LANES), F32), jax.ShapeDtypeStruct((t, SG_WIDTH), BF16)],
        compiler_params=pltpu.CompilerParams(dimension_semantics=("arbitrary",),
                                             vmem_limit_bytes=VMEM_LIMIT_BYTES),
        name="inproj_sg",
    )(x2, ng, wqkvz, wab, wsg, lng, lnb, ws, bst, og)


def _pair(a, b):
    return jnp.concatenate([a, b], axis=1)


def _blockdiag(p):
    n = p.shape[0]
    zero = jnp.zeros((n, n), p.dtype)
    return jnp.concatenate([jnp.concatenate([p[:, :n], zero], axis=1),
                            jnp.concatenate([zero, p[:, n:]], axis=1)], axis=0)


def _pdot(p, q):
    return jnp.dot(p.astype(BF16), _blockdiag(q.astype(BF16)), preferred_element_type=F32)


def _unit_tri_inverse(a2s, row, colm):
    eye = (row == colm).astype(F32)
    base = 8
    base_mask = (row // base) == (colm // base)
    ld = [jnp.where(base_mask, a2, 0.0) for a2 in a2s]
    ld2 = [_pdot(x, x) for x in ld]
    ld4 = [_pdot(x, x) for x in ld2]
    t = [eye - x for x in ld]
    t = [ti + _pdot(ti, x) for ti, x in zip(t, ld2)]
    t = [ti + _pdot(ti, x) for ti, x in zip(t, ld4)]
    m = base
    while m < DN_CHUNK:
        joins = ((row // (2 * m)) == (colm // (2 * m))) & ((row // m) != (colm // m))
        x = [_pdot(jnp.where(joins, a2, 0.0), ti) for a2, ti in zip(a2s, t)]
        t = [ti - _pdot(ti, xi) for ti, xi in zip(t, x)]
        m *= 2
    return t


def _chunk_cumsum(x, rows, reverse):
    n = x.shape[0]
    s = 1
    while s < n:
        if reverse:
            x = x + jnp.where(rows < n - s, pltpu.roll(x, n - s, 0), 0.0)
        else:
            x = x + jnp.where(rows >= s, pltpu.roll(x, s, 0), 0.0)
        s *= 2
    return x


def _delta_operands(reverse, chunk, n_chunks, x, xp, xn, gates, cw_ref, alog_ref, dtb_ref):
    c = DN_CHUNK
    row = lax.broadcasted_iota(jnp.int32, (c, c), 0)
    col = lax.broadcasted_iota(jnp.int32, (c, c), 1)

    x = x.astype(F32)
    rows_x = lax.broadcasted_iota(jnp.int32, x.shape, 0)
    prev_row = xp[BF16_SUBLANES - 1:BF16_SUBLANES, :].astype(F32) * (chunk > 0).astype(F32)
    next_row = xn[0:1, :].astype(F32) * (chunk < n_chunks - 1).astype(F32)
    x_prev = jnp.where(rows_x == 0, prev_row, pltpu.roll(x, 1, 0))
    x_next = jnp.where(rows_x == c - 1, next_row, pltpu.roll(x, c - 1, 0))
    qkv = _silu(cw_ref[0:1, :] * x_prev + cw_ref[1:2, :] * x + cw_ref[2:3, :] * x_next)

    z = gates + dtb_ref[...]
    softplus = jnp.maximum(z, 0.0) + jnp.log(1.0 + jnp.exp(-jnp.abs(z)))
    g = -jnp.exp(alog_ref[...]) * softplus
    beta = jax.nn.sigmoid(gates)
    gc = _chunk_cumsum(g, row, reverse)
    gct = gc.T
    g_last = gc[0:1, :] if reverse else gc[c - 1:c, :]
    g_last_t = gct[:, 0:1] if reverse else gct[:, c - 1:c]
    e_gc = jnp.exp(gc)
    e_tail_t = jnp.exp(g_last_t - gct)
    e_last = jnp.exp(g_last)

    incl = (row <= col) if reverse else (row >= col)
    strict = (row < col) if reverse else (row > col)
    d = 1 if reverse else 0

    pairs = []
    for pair_idx in range(DN_HEADS // 2):
        kbs, qs, qdecs, kts, ktails, rhss, decays, e_lasts = [], [], [], [], [], [], [], []
        for h in (2 * pair_idx, 2 * pair_idx + 1):
            r = d * DN_HEADS + h
            q = qkv[:, h * DN_HEAD_DIM:(h + 1) * DN_HEAD_DIM]
            k = qkv[:, DN_WIDTH + h * DN_HEAD_DIM:DN_WIDTH + (h + 1) * DN_HEAD_DIM]
            v = qkv[:, 2 * DN_WIDTH + h * DN_HEAD_DIM:2 * DN_WIDTH + (h + 1) * DN_HEAD_DIM]
            q = q * (lax.rsqrt(jnp.sum(q * q, axis=-1, keepdims=True) + EPS) * (DN_HEAD_DIM ** -0.5))
            k = k * lax.rsqrt(jnp.sum(k * k, axis=-1, keepdims=True) + EPS)
            b_col = beta[:, 2 * DN_HEADS + r:2 * DN_HEADS + r + 1]
            eg_col = e_gc[:, r:r + 1]
            kt = k.T
            kb = k * b_col
            decays.append(jnp.exp(jnp.where(incl, gc[:, r:r + 1] - gct[r:r + 1, :], -1e30)))
            kbs.append(kb)
            qs.append(q)
            qdecs.append(q * eg_col)
            kts.append(kt)
            ktails.append(kt * e_tail_t[r:r + 1, :])
            rhss.append(jnp.concatenate([v * b_col, kb * eg_col], axis=1))
            e_lasts.append(jnp.broadcast_to(e_last[:, r:r + 1], (1, DN_HEAD_DIM)))
        pairs.append(dict(
            gram_lhs=jnp.concatenate([_pair(*kbs), _pair(*qs)], axis=0).astype(BF16),
            kt2=_pair(*kts).astype(BF16), decay2=_pair(*decays), strict2=_pair(strict, strict),
            rhss=[r_.astype(BF16) for r_ in rhss], qdec2=_pair(*qdecs).astype(BF16),
            ktail2=_pair(*ktails).astype(BF16), e_last2=_pair(*e_lasts)))
    return pairs


def _delta_chunks(pairs, states):
    c = DN_CHUNK
    row2 = lax.broadcasted_iota(jnp.int32, (c, 2 * c), 0)
    colm2 = lax.broadcasted_iota(jnp.int32, (c, 2 * c), 1) % c
    grams = [_pdot(p["gram_lhs"], p["kt2"]) for p in pairs]
    a2s = [jnp.where(p["strict2"], gm[:c] * p["decay2"], 0.0) for p, gm in zip(pairs, grams)]
    attn2s = [gm[c:] * p["decay2"] for p, gm in zip(pairs, grams)]
    t2s = _unit_tri_inverse(a2s, row2, colm2)
    uws = [[_bdot(t2[:, i * c:(i + 1) * c], p["rhss"][i]) for i in range(2)] for p, t2 in zip(pairs, t2s)]
    u2s = [_pair(uw[0][:, :DN_HEAD_DIM], uw[1][:, :DN_HEAD_DIM]) for uw in uws]
    w2s = [_pair(uw[0][:, DN_HEAD_DIM:], uw[1][:, DN_HEAD_DIM:]) for uw in uws]
    rss = [_pdot(jnp.concatenate([w2.astype(BF16), p["qdec2"]], axis=0), s2)
           for p, w2, s2 in zip(pairs, w2s, states)]
    v_news = [u2 - rs[:c] for u2, rs in zip(u2s, rss)]
    rvs = [_pdot(jnp.concatenate([attn2.astype(BF16), p["ktail2"]], axis=0), vn)
           for p, attn2, vn in zip(pairs, attn2s, v_news)]
    outs = [rs[c:] + rv[:c] for rs, rv in zip(rss, rvs)]
    new_states = [s2 * p["e_last2"] + rv[c:] for p, s2, rv in zip(pairs, states, rvs)]
    return outs, new_states


def _delta_kernel(xf_ref, xfp_ref, xfn_ref, xb_ref, xbp_ref, xbn_ref, gf_ref, gb_ref, cw_ref, alog_ref,
                  dtb_ref, of_ref, ob_ref, s_ref):
    n = pl.program_id(1)
    n_chunks = pl.num_programs(1)
    n_rows = xf_ref.shape[0]
    half = DN_HEADS // 2

    @pl.when(n == 0)
    def _():
        s_ref[...] = jnp.zeros_like(s_ref)

    pairs = []
    for b in range(n_rows):
        pairs += _delta_operands(False, n, n_chunks, xf_ref[b], xfp_ref[b], xfn_ref[b], gf_ref[b],
                                 cw_ref, alog_ref, dtb_ref)
        pairs += _delta_operands(True, n_chunks - 1 - n, n_chunks, xb_ref[b], xbp_ref[b], xbn_ref[b],
                                 gb_ref[b], cw_ref, alog_ref, dtb_ref)
    outs, new_states = _delta_chunks(pairs, [s_ref[i] for i in range(len(pairs))])
    for i, (o, s2) in enumerate(zip(outs, new_states)):
        b, d, pair_idx = i // (2 * half), (i // half) % 2, i % half
        o_ref = ob_ref if d else of_ref
        o_ref[b, :, 2 * pair_idx * DN_HEAD_DIM:(2 * pair_idx + 2) * DN_HEAD_DIM] = o
        s_ref[i] = s2


def _delta_rule(qkv, gates, conv_w, alog_row, dtb_row):
    bsz, seq, width = qkv.shape
    c = DN_CHUNK
    nb = DN_BATCH_ROWS
    nc = seq // c
    halo = BF16_SUBLANES
    per_chunk = c // halo
    n_halo = seq // halo

    fwd = lambda n: n
    bwd = lambda n: nc - 1 - n

    def specs(chunk_of):
        return [
            pl.BlockSpec((nb, c, width), lambda b, n: (b, chunk_of(n), 0)),
            pl.BlockSpec((nb, halo, width),
                         lambda b, n: (b, jnp.maximum(chunk_of(n) * per_chunk - 1, 0), 0)),
            pl.BlockSpec((nb, halo, width),
                         lambda b, n: (b, jnp.minimum((chunk_of(n) + 1) * per_chunk, n_halo - 1), 0)),
        ]

    gate_spec = lambda chunk_of: pl.BlockSpec((nb, c, LANES), lambda b, n: (b, chunk_of(n), 0))
    out_spec = lambda chunk_of: pl.BlockSpec((nb, c, DN_WIDTH), lambda b, n: (b, chunk_of(n), 0))
    full = lambda *shape: pl.BlockSpec(shape, lambda b, n: (0,) * len(shape))
    out_sds = jax.ShapeDtypeStruct((bsz, seq, DN_WIDTH), F32)
    return pl.pallas_call(
        _delta_kernel,
        grid=(bsz // nb, nc),
        in_specs=specs(fwd) + specs(bwd) + [gate_spec(fwd), gate_spec(bwd), full(3, width),
                                            full(1, LANES), full(1, LANES)],
        out_specs=[out_spec(fwd), out_spec(bwd)],
        out_shape=[out_sds, out_sds],
        scratch_shapes=[pltpu.VMEM((nb * DN_HEADS, DN_HEAD_DIM, 2 * DN_HEAD_DIM), F32)],
        compiler_params=pltpu.CompilerParams(dimension_semantics=("arbitrary", "arbitrary"),
                                             vmem_limit_bytes=VMEM_LIMIT_BYTES),
        name="delta_rule",
    )(qkv, qkv, qkv, qkv, qkv, qkv, gates, gates, conv_w, alog_row, dtb_row)


def _mix_ffn_kernel(x_ref, of_ref, ob_ref, z_ref, yb_ref, dng_ref, wo_ref, fg_ref, wg_ref, wu_ref, wd_ref,
                    fin_ref, out_ref, *, final):
    o = of_ref[...] + ob_ref[...]
    z = z_ref[...].astype(F32)
    heads = []
    for h in range(DN_HEADS):
        cols = slice(h * DN_HEAD_DIM, (h + 1) * DN_HEAD_DIM)
        heads.append((_rms(o[:, cols], dng_ref[...]) * _silu(z[:, cols])).astype(BF16))
    ya = jnp.concatenate(heads, axis=1)
    x1 = (x_ref[...] + jnp.dot(ya, wo_ref[:DN_WIDTH, :], preferred_element_type=F32)
          + jnp.dot(yb_ref[...], wo_ref[DN_WIDTH:, :], preferred_element_type=F32))
    hb = _rms(x1, fg_ref[...]).astype(BF16)
    gate = jnp.dot(hb, wg_ref[...], preferred_element_type=F32)
    up = jnp.dot(hb, wu_ref[...], preferred_element_type=F32)
    hid = (_silu(gate) * up).astype(BF16)
    out = x1 + jnp.dot(hid, wd_ref[...], preferred_element_type=F32)
    if final:
        out = _rms(out, fin_ref[...])
    out_ref[...] = out


def _mix_ffn(x2, o_f, o_b, z, yb, dng, wo, fg, wg, wu, wd, fin, final):
    t, d = x2.shape
    f = wg.shape[1]
    tm = TOKEN_TILE
    once = pl.Buffered(1)
    full = lambda *shape: pl.BlockSpec(shape, lambda i: (0,) * len(shape), pipeline_mode=once)
    rows = lambda w: pl.BlockSpec((tm, w), lambda i: (i, 0))
    return pl.pallas_call(
        functools.partial(_mix_ffn_kernel, final=final),
        grid=(t // tm,),
        in_specs=[rows(d), rows(DN_WIDTH), rows(DN_WIDTH), rows(DN_WIDTH), rows(SG_WIDTH),
                  full(1, DN_HEAD_DIM), full(DN_WIDTH + SG_WIDTH, d), full(1, d), full(d, f), full(d, f),
                  full(f, d), full(1, d)],
        out_specs=rows(d),
        out_shape=jax.ShapeDtypeStruct((t, d), F32),
        compiler_params=pltpu.CompilerParams(dimension_semantics=("arbitrary",),
                                             vmem_limit_bytes=VMEM_LIMIT_BYTES),
        name="mix_ffn",
    )(x2, o_f, o_b, z, yb, dng, wo, fg, wg, wu, wd, fin)


def _pad_lanes(v):
    flat = v.reshape(1, -1).astype(F32)
    return jnp.pad(flat, ((0, 0), (0, LANES - flat.shape[1])))


def kernel(x, mix_norm_g, w_in, conv_w, dn_a_log, dn_dt_bias, dn_norm_g, sg_ln_g, sg_ln_b, sg_w, sg_b,
           sg_out_g, w_out, ffn_norm_g, w_gate, w_up, w_down, final_norm_g):
    bsz, seq, d = x.shape
    depth = w_in.shape[0]
    assert seq % TOKEN_TILE == 0 and seq % DN_CHUNK == 0 and TOKEN_TILE % SG_CHUNK == 0
    assert bsz % DN_BATCH_ROWS == 0
    x2 = x.reshape(bsz * seq, d)
    row = lambda v: v.reshape(1, -1).astype(F32)
    for l in range(depth):
        wl = w_in[l]
        wqkvz = wl[:, :OFF_A].astype(BF16)
        wab = jnp.pad(wl[:, OFF_A:OFF_SG], ((0, 0), (0, LANES - (OFF_SG - OFF_A)))).astype(BF16)
        wsg = wl[:, OFF_SG:].astype(BF16)
        qkv, z, gates, yb = _inproj(x2, row(mix_norm_g[l]), wqkvz, wab, wsg, row(sg_ln_g[l]),
                                    row(sg_ln_b[l]), sg_w[l].astype(BF16), sg_b[l].T.astype(F32),
                                    row(sg_out_g[l]))
        o_f, o_b = _delta_rule(qkv.reshape(bsz, seq, OFF_Z), gates.reshape(bsz, seq, LANES),
                               conv_w[l].astype(F32), _pad_lanes(dn_a_log[l]), _pad_lanes(dn_dt_bias[l]))
        x2 = _mix_ffn(x2, o_f.reshape(bsz * seq, DN_WIDTH), o_b.reshape(bsz * seq, DN_WIDTH), z, yb,
                      row(dn_norm_g[l]), w_out[l].astype(BF16), row(ffn_norm_g[l]),
                      w_gate[l].astype(BF16), w_up[l].astype(BF16), w_down[l].astype(BF16),
                      row(final_norm_g), final=(l == depth - 1))
    return x2.reshape(bsz, seq, d)
```

```python
import functools

import jax
import jax.numpy as jnp
from jax import lax
from jax.experimental import pallas as pl
from jax.experimental.pallas import tpu as pltpu

F32 = jnp.float32
BF16 = jnp.bfloat16

DN_HEADS = 4
DN_HEAD_DIM = 128
DN_WIDTH = DN_HEADS * DN_HEAD_DIM
SG_GROUPS = 4
SG_GROUP_DIM = 128
SG_WIDTH = SG_GROUPS * SG_GROUP_DIM
SG_CHUNK = 128
EPS = 1e-6

OFF_Z = 3 * DN_WIDTH
OFF_A = 4 * DN_WIDTH
OFF_B = OFF_A + 2 * DN_HEADS
OFF_SG = OFF_B + 2 * DN_HEADS

LANES = 128
F32_SUBLANES = 8
VMEM_LIMIT_BYTES = 56 * 1024 * 1024

DN_CHUNK = 128
NEUMANN_BLOCK = 8
DN_BATCH_ROWS = 4
TOKEN_TILE = 512
INPROJ_SUBTILES = 2


def _rms(x, gain):
    return x * lax.rsqrt(jnp.mean(x * x, axis=-1, keepdims=True) + EPS) * gain


def _silu(x):
    return x * jax.nn.sigmoid(x)


def _gelu_tanh(x):
    c = 0.7978845608028654
    return x * (0.5 * (1.0 + jnp.tanh(c * (x + 0.044715 * (x * x * x)))))


def _bdot(a, b):
    return jnp.dot(a.astype(BF16), b.astype(BF16), preferred_element_type=F32)


def _conv_silu_norm(pre, prev_row, next_row, cw_ref, qkv_ref, rows):
    n = pre.shape[0]
    sub = F32_SUBLANES
    first = lax.broadcasted_iota(jnp.int32, (sub, pre.shape[1]), 0) == 0
    last = lax.broadcasted_iota(jnp.int32, (sub, pre.shape[1]), 0) == sub - 1
    x_prev = pltpu.roll(pre, 1, 0)
    x_prev = jnp.concatenate([jnp.where(first, prev_row, x_prev[:sub]), x_prev[sub:]], axis=0)
    x_next = pltpu.roll(pre, n - 1, 0)
    x_next = jnp.concatenate([x_next[:n - sub], jnp.where(last, next_row, x_next[n - sub:])], axis=0)
    qkv = _silu(cw_ref[0:1, :] * x_prev + cw_ref[1:2, :] * pre + cw_ref[2:3, :] * x_next)
    for h in range(2 * DN_HEADS):
        cols = slice(h * DN_HEAD_DIM, (h + 1) * DN_HEAD_DIM)
        t = qkv[:, cols]
        scale = DN_HEAD_DIM ** -0.5 if h < DN_HEADS else 1.0
        qkv_ref[rows, cols] = (t * (lax.rsqrt(jnp.sum(t * t, axis=-1, keepdims=True) + EPS) * scale)
                               ).astype(BF16)
    qkv_ref[rows, 2 * DN_WIDTH:] = qkv[:, 2 * DN_WIDTH:].astype(BF16)


def _spatial_gating(ps, lng_ref, lnb_ref, ws_ref, bst_ref, og_ref, yb_ref, row0):
    ps = _gelu_tanh(ps)
    u = ps[:, :SG_WIDTH]
    v = ps[:, SG_WIDTH:]
    vc = v - jnp.mean(v, axis=-1, keepdims=True)
    v = vc * lax.rsqrt(jnp.mean(vc * vc, axis=-1, keepdims=True) + EPS) * lng_ref[...] + lnb_ref[...]
    vb = v.astype(BF16)
    for n in range(ps.shape[0] // SG_CHUNK):
        rows = slice(n * SG_CHUNK, (n + 1) * SG_CHUNK)
        for g in range(SG_GROUPS):
            cols = slice(g * SG_GROUP_DIM, (g + 1) * SG_GROUP_DIM)
            mixed = jnp.dot(ws_ref[g], vb[rows, cols], preferred_element_type=F32) + bst_ref[:, g:g + 1]
            y = _rms(u[rows, cols] * mixed, og_ref[:, cols])
            yb_ref[row0 + n * SG_CHUNK:row0 + (n + 1) * SG_CHUNK, cols] = y.astype(BF16)


def _inproj_kernel(x_ref, xprev_ref, xnext_ref, ng_ref, wqkvz_ref, wab_ref, wsg_ref, lng_ref, lnb_ref,
                   ws_ref, bst_ref, og_ref, cw_ref, qkv_ref, z_ref, gates_ref, yb_ref, *, tiles_per_seq):
    tm = x_ref.shape[0]
    st = tm // INPROJ_SUBTILES
    halo = xprev_ref.shape[0]
    sub_rows = [slice(s * st, (s + 1) * st) for s in range(INPROJ_SUBTILES)]
    h_halo = _rms(jnp.concatenate([xprev_ref[...], xnext_ref[...]], axis=0), ng_ref[...]).astype(BF16)
    hs = [_rms(x_ref[r, :], ng_ref[...]).astype(BF16) for r in sub_rows]
    pq0 = jnp.dot(jnp.concatenate([hs[0], h_halo], axis=0), wqkvz_ref[...], preferred_element_type=F32)
    pqs = [pq0[:st]] + [jnp.dot(h, wqkvz_ref[...], preferred_element_type=F32) for h in hs[1:]]
    pss = [jnp.dot(h, wsg_ref[...], preferred_element_type=F32) for h in hs]
    gates_ref[...] = jnp.dot(jnp.concatenate(hs, axis=0), wab_ref[...],
                             preferred_element_type=F32)

    pos = pl.program_id(0) % tiles_per_seq
    tile_prev = pq0[st + halo - 1:st + halo, :OFF_Z] * (pos > 0).astype(F32)
    tile_next = pq0[st + halo:st + halo + 1, :OFF_Z] * (pos < tiles_per_seq - 1).astype(F32)
    for s, r in enumerate(sub_rows):
        z_ref[r, :] = pqs[s][:, OFF_Z:].astype(BF16)
        prev_row = tile_prev if s == 0 else pqs[s - 1][st - 1:st, :OFF_Z]
        next_row = tile_next if s == INPROJ_SUBTILES - 1 else pqs[s + 1][0:1, :OFF_Z]
        _conv_silu_norm(pqs[s][:, :OFF_Z], prev_row, next_row, cw_ref, qkv_ref, r)
    for s in range(INPROJ_SUBTILES):
        _spatial_gating(pss[s], lng_ref, lnb_ref, ws_ref, bst_ref, og_ref, yb_ref, s * st)


def _inproj(x2, ng, wqkvz, wab, wsg, lng, lnb, ws, bst, og, conv_w, seq):
    t, d = x2.shape
    tm = TOKEN_TILE
    halo = F32_SUBLANES
    per_tile = tm // halo
    n_halo = t // halo
    full = lambda *shape: pl.BlockSpec(shape, lambda i: (0,) * len(shape))
    rows = lambda w: pl.BlockSpec((tm, w), lambda i: (i, 0))
    prev_spec = pl.BlockSpec((halo, d), lambda i: (jnp.maximum(i * per_tile - 1, 0), 0))
    next_spec = pl.BlockSpec((halo, d), lambda i: (jnp.minimum((i + 1) * per_tile, n_halo - 1), 0))
    return pl.pallas_call(
        functools.partial(_inproj_kernel, tiles_per_seq=seq // tm),
        grid=(t // tm,),
        in_specs=[rows(d), prev_spec, next_spec, full(1, d), full(d, 4 * DN_WIDTH), full(d, LANES),
                  full(d, 2 * SG_WIDTH), full(1, SG_WIDTH), full(1, SG_WIDTH),
                  full(SG_GROUPS, SG_CHUNK, SG_CHUNK), full(SG_CHUNK, SG_GROUPS), full(1, SG_WIDTH),
                  full(3, OFF_Z)],
        out_specs=[rows(OFF_Z), rows(DN_WIDTH), rows(LANES), rows(SG_WIDTH)],
        out_shape=[jax.ShapeDtypeStruct((t, OFF_Z), BF16), jax.ShapeDtypeStruct((t, DN_WIDTH), BF16),
                   jax.ShapeDtypeStruct((t, LANES), F32), jax.ShapeDtypeStruct((t, SG_WIDTH), BF16)],
        compiler_params=pltpu.CompilerParams(dimension_semantics=("arbitrary",),
                                             vmem_limit_bytes=VMEM_LIMIT_BYTES),
        name="inproj_sg",
    )(x2, x2, x2, ng, wqkvz, wab, wsg, lng, lnb, ws, bst, og, conv_w)


def _pair(a, b):
    return jnp.concatenate([a, b], axis=1)


def _blockdiag(p):
    n = p.shape[0]
    zero = jnp.zeros((n, n), p.dtype)
    return jnp.concatenate([jnp.concatenate([p[:, :n], zero], axis=1),
                            jnp.concatenate([zero, p[:, n:]], axis=1)], axis=0)


def _pdot(p, q):
    return jnp.dot(p.astype(BF16), _blockdiag(q.astype(BF16)), preferred_element_type=F32)


def _unit_tri_inverse(a2s):
    n = a2s[0].shape[0]
    row = lax.broadcasted_iota(jnp.int32, (n, 2 * n), 0)
    colm = lax.broadcasted_iota(jnp.int32, (n, 2 * n), 1) % n
    eye = (row == colm).astype(F32)
    base = NEUMANN_BLOCK
    base_mask = (row // base) == (colm // base)
    ld = [jnp.where(base_mask, a2, 0.0) for a2 in a2s]
    ld2 = [_pdot(x, x) for x in ld]
    ld4 = [_pdot(x, x) for x in ld2]
    t = [eye - x for x in ld]
    t = [ti + _pdot(ti, x) for ti, x in zip(t, ld2)]
    t = [ti + _pdot(ti, x) for ti, x in zip(t, ld4)]
    m = base
    while m < n:
        joins =((row // (2 * m)) == (colm // (2 * m))) & ((row // m) != (colm // m))
        x = [_pdot(jnp.where(joins, a2, 0.0), ti) for a2, ti in zip(a2s, t)]
        t = [ti - _pdot(ti, xi) for ti, xi in zip(t, x)]
        m *= 2
    return t


def _chunk_cumsum(x, rows, reverse):
    n = x.shape[0]
    s = 1
    while s < n:
        if reverse:
            x = x + jnp.where(rows < n - s, pltpu.roll(x, n - s, 0), 0.0)
        else:
            x = x + jnp.where(rows >= s, pltpu.roll(x, s, 0), 0.0)
        s *= 2
    return x


def _delta_operands(reverse, qkv, gates, alog_ref, dtb_ref):
    c = DN_CHUNK
    row = lax.broadcasted_iota(jnp.int32, (c, c), 0)
    col = lax.broadcasted_iota(jnp.int32, (c, c), 1)
    qkv = qkv.astype(F32)

    z = gates + dtb_ref[...]
    softplus = jnp.maximum(z, 0.0) + jnp.log(1.0 + jnp.exp(-jnp.abs(z)))
    g = -jnp.exp(alog_ref[...]) * softplus
    beta = jax.nn.sigmoid(gates)
    gc = _chunk_cumsum(g, row, reverse)
    gct = gc.T
    g_last = gc[0:1, :] if reverse else gc[c - 1:c, :]
    g_last_t = gct[:, 0:1] if reverse else gct[:, c - 1:c]
    e_gc = jnp.exp(gc)
    e_tail_t = jnp.exp(g_last_t - gct)
    e_last = jnp.exp(g_last)

    incl = (row <= col) if reverse else (row >= col)
    strict = (row < col) if reverse else (row > col)
    d = 1 if reverse else 0

    pairs = []
    for pair_idx in range(DN_HEADS // 2):
        kbs, qs, qdecs, kts, ktails, rhss, decays, e_lasts = [], [], [], [], [], [], [], []
        for h in (2 * pair_idx, 2 * pair_idx + 1):
            r = d * DN_HEADS + h
            q = qkv[:, h * DN_HEAD_DIM:(h + 1) * DN_HEAD_DIM]
            k = qkv[:, DN_WIDTH + h * DN_HEAD_DIM:DN_WIDTH + (h + 1) * DN_HEAD_DIM]
            v = qkv[:, 2 * DN_WIDTH + h * DN_HEAD_DIM:2 * DN_WIDTH + (h + 1) * DN_HEAD_DIM]
            b_col = beta[:, 2 * DN_HEADS + r:2 * DN_HEADS + r + 1]
            eg_col = e_gc[:, r:r + 1]
            kt = k.T
            kb = k * b_col
            decays.append(jnp.exp(jnp.where(incl, gc[:, r:r + 1] - gct[r:r + 1, :], -1e30)))
            kbs.append(kb)
            qs.append(q)
            qdecs.append(q * eg_col)
            kts.append(kt)
            ktails.append(kt * e_tail_t[r:r + 1, :])
            rhss.append(jnp.concatenate([v * b_col, kb * eg_col], axis=1))
            e_lasts.append(jnp.broadcast_to(e_last[:, r:r + 1], (1, DN_HEAD_DIM)))
        pairs.append(dict(
            reverse=reverse,
            gram_lhs=jnp.concatenate([_pair(*kbs), _pair(*qs)], axis=0).astype(BF16),
            kt2=_pair(*kts).astype(BF16), decay2=_pair(*decays), strict2=_pair(strict, strict),
            rhss=rhss, qdec2=_pair(*qdecs).astype(BF16),
            ktail2=_pair(*ktails).astype(BF16), e_last2=_pair(*e_lasts)))
    return pairs


def _wy_solve(a_list, rhs_list, rev_list):
    c = DN_CHUNK
    hc = c // 2
    lane = lax.broadcasted_iota(jnp.int32, (hc, c), 1)
    diag = [jnp.where(lane < hc, a[:hc], a[hc:]) for a in a_list]
    t12 = _unit_tri_inverse(diag)
    t21 = [pltpu.roll(t, hc, 1) for t in t12]
    t_first = [(t21 if rev else t12)[i][:, :hc] for i, rev in enumerate(rev_list)]
    t_second = [(t12 if rev else t21)[i][:, :hc] for i, rev in enumerate(rev_list)]
    a_off = [pltpu.roll(a[:hc], hc, 1)[:, :hc] if rev else a[hc:, :hc] for a, rev in zip(a_list, rev_list)]
    r_first = [r_[hc:] if rev else r_[:hc] for r_, rev in zip(rhs_list, rev_list)]
    r_second = [r_[:hc] if rev else r_[hc:] for r_, rev in zip(rhs_list, rev_list)]
    x_first = [_bdot(t, r_) for t, r_ in zip(t_first, r_first)]
    y = [_bdot(a, x) for a, x in zip(a_off, x_first)]
    x_second = [_bdot(t, r_ - y_) for t, r_, y_ in zip(t_second, r_second, y)]
    return [jnp.concatenate([x2, x1] if rev else [x1, x2], axis=0)
            for x1, x2, rev in zip(x_first, x_second, rev_list)]


def _delta_chunks(pairs, states):
    c = DN_CHUNK
    grams = [_pdot(p["gram_lhs"], p["kt2"]) for p in pairs]
    a2s = [jnp.where(p["strict2"], gm[:c] * p["decay2"], 0.0) for p, gm in zip(pairs, grams)]
    attn2s = [gm[c:] * p["decay2"] for p, gm in zip(pairs, grams)]
    uw = _wy_solve([a2[:, i * c:(i + 1) * c] for a2 in a2s for i in range(2)],
                   [p["rhss"][i] for p in pairs for i in range(2)],
                   [p["reverse"] for p in pairs for i in range(2)])
    u2s = [_pair(uw[2 * j][:, :DN_HEAD_DIM], uw[2 * j + 1][:, :DN_HEAD_DIM]) for j in range(len(pairs))]
    w2s = [_pair(uw[2 * j][:, DN_HEAD_DIM:], uw[2 * j + 1][:, DN_HEAD_DIM:]) for j in range(len(pairs))]
    rss = [_pdot(jnp.concatenate([w2.astype(BF16), p["qdec2"]], axis=0), s2)
           for p, w2, s2 in zip(pairs, w2s, states)]
    v_news = [u2 - rs[:c] for u2, rs in zip(u2s, rss)]
    rvs = [_pdot(jnp.concatenate([attn2.astype(BF16), p["ktail2"]], axis=0), vn)
           for p, attn2, vn in zip(pairs, attn2s, v_news)]
    outs = [rs[c:] + rv[:c] for rs, rv in zip(rss, rvs)]
    new_states = [s2 * p["e_last2"] + rv[c:] for p, s2, rv in zip(pairs, states, rvs)]
    return outs, new_states


def _delta_kernel(xf_ref, xb_ref, gf_ref, gb_ref, alog_ref, dtb_ref, of_ref, ob_ref, s_ref):
    n_rows = xf_ref.shape[0]
    half = DN_HEADS // 2

    @pl.when(pl.program_id(1) == 0)
    def _():
        s_ref[...] = jnp.zeros_like(s_ref)

    pairs = []
    for b in range(n_rows):
        pairs += _delta_operands(False, xf_ref[b], gf_ref[b], alog_ref, dtb_ref)
        pairs += _delta_operands(True, xb_ref[b], gb_ref[b], alog_ref, dtb_ref)
    outs, new_states = _delta_chunks(pairs, [s_ref[i] for i in range(len(pairs))])
    for i, (o, s2) in enumerate(zip(outs, new_states)):
        b, d, pair_idx = i // (2 * half), (i // half) % 2, i % half
        o_ref = ob_ref if d else of_ref
        o_ref[b, :, 2 * pair_idx * DN_HEAD_DIM:(2 * pair_idx + 2) * DN_HEAD_DIM] = o
        s_ref[i] = s2


def _delta_rule(qkv, gates, alog_row, dtb_row):
    bsz, seq, width = qkv.shape
    c = DN_CHUNK
    nb = DN_BATCH_ROWS
    nc = seq // c
    fwd = lambda n: n
    bwd = lambda n: nc - 1 - n
    chunk_spec = lambda w, chunk_of: pl.BlockSpec((nb, c, w), lambda b, n: (b, chunk_of(n), 0))
    full = lambda *shape: pl.BlockSpec(shape, lambda b, n: (0,) * len(shape))
    out_sds = jax.ShapeDtypeStruct((bsz, seq, DN_WIDTH), F32)
    return pl.pallas_call(
        _delta_kernel,
        grid=(bsz // nb, nc),
        in_specs=[chunk_spec(width, fwd), chunk_spec(width, bwd), chunk_spec(LANES, fwd),
                  chunk_spec(LANES, bwd), full(1, LANES), full(1, LANES)],
        out_specs=[chunk_spec(DN_WIDTH, fwd), chunk_spec(DN_WIDTH, bwd)],
        out_shape=[out_sds, out_sds],
        scratch_shapes=[pltpu.VMEM((nb * DN_HEADS, DN_HEAD_DIM, 2 * DN_HEAD_DIM), F32)],
        compiler_params=pltpu.CompilerParams(dimension_semantics=("arbitrary", "arbitrary"),
                                             vmem_limit_bytes=VMEM_LIMIT_BYTES),
        name="delta_rule",
    )(qkv, qkv, gates, gates, alog_row, dtb_row)


def _mix_ffn_kernel(x_ref, of_ref, ob_ref, z_ref, yb_ref, dng_ref, wo_ref, fg_ref, wg_ref, wu_ref, wd_ref,
                    fin_ref, out_ref, *, final):
    o = of_ref[...] + ob_ref[...]
    z = z_ref[...].astype(F32)
    heads = []
    for h in range(DN_HEADS):
        cols = slice(h * DN_HEAD_DIM, (h + 1) * DN_HEAD_DIM)
        heads.append((_rms(o[:, cols], dng_ref[...]) * _silu(z[:, cols])).astype(BF16))
    ya = jnp.concatenate(heads, axis=1)
    x1 = (x_ref[...] + jnp.dot(ya, wo_ref[:DN_WIDTH, :], preferred_element_type=F32)
          + jnp.dot(yb_ref[...], wo_ref[DN_WIDTH:, :], preferred_element_type=F32))
    hb = _rms(x1, fg_ref[...]).astype(BF16)
    gate = jnp.dot(hb, wg_ref[...], preferred_element_type=F32)
    up = jnp.dot(hb, wu_ref[...], preferred_element_type=F32)
    hid = (_silu(gate) * up).astype(BF16)
    out = x1 + jnp.dot(hid, wd_ref[...], preferred_element_type=F32)
    if final:
        out = _rms(out, fin_ref[...])
    out_ref[...] = out


def _mix_ffn(x2, o_f, o_b, z, yb, dng, wo, fg, wg, wu, wd, fin, final):
    t, d = x2.shape
    f = wg.shape[1]
    tm = TOKEN_TILE
    once = pl.Buffered(1)
    full = lambda *shape: pl.BlockSpec(shape, lambda i: (0,) * len(shape), pipeline_mode=once)
    rows = lambda w: pl.BlockSpec((tm, w), lambda i: (i, 0))
    return pl.pallas_call(
        functools.partial(_mix_ffn_kernel, final=final),
        grid=(t // tm,),
        in_specs=[rows(d), rows(DN_WIDTH), rows(DN_WIDTH), rows(DN_WIDTH), rows(SG_WIDTH),
                  full(1, DN_HEAD_DIM), full(DN_WIDTH + SG_WIDTH, d), full(1, d), full(d, f), full(d, f),
                  full(f, d), full(1, d)],
        out_specs=rows(d),
        out_shape=jax.ShapeDtypeStruct((t, d), F32),
        compiler_params=pltpu.CompilerParams(dimension_semantics=("arbitrary",),
                                             vmem_limit_bytes=VMEM_LIMIT_BYTES),
        name="mix_ffn",
    )(x2, o_f, o_b, z, yb, dng, wo, fg, wg, wu, wd, fin)


def _pad_lanes(v):
    flat = v.reshape(1, -1).astype(F32)
    return jnp.pad(flat, ((0, 0), (0, LANES - flat.shape[1])))


def kernel(x, mix_norm_g, w_in, conv_w, dn_a_log, dn_dt_bias, dn_norm_g, sg_ln_g, sg_ln_b, sg_w, sg_b,
           sg_out_g, w_out, ffn_norm_g, w_gate, w_up, w_down, final_norm_g):
    bsz, seq, d = x.shape
    depth = w_in.shape[0]
    assert seq % TOKEN_TILE == 0 and seq % DN_CHUNK == 0 and TOKEN_TILE % SG_CHUNK == 0
    assert bsz % DN_BATCH_ROWS == 0
    x2 = x.reshape(bsz * seq, d)
    row = lambda v: v.reshape(1, -1).astype(F32)
    for l in range(depth):
        wl = w_in[l]
        wqkvz = wl[:, :OFF_A].astype(BF16)
        wab = jnp.pad(wl[:, OFF_A:OFF_SG], ((0, 0), (0, LANES - (OFF_SG - OFF_A)))).astype(BF16)
        wsg = wl[:, OFF_SG:].astype(BF16)
        qkv, z, gates, yb = _inproj(x2, row(mix_norm_g[l]), wqkvz, wab, wsg, row(sg_ln_g[l]),
                                    row(sg_ln_b[l]), sg_w[l].astype(BF16), sg_b[l].T.astype(F32),
                                    row(sg_out_g[l]), conv_w[l].astype(F32), seq)
        o_f, o_b = _delta_rule(qkv.reshape(bsz, seq, OFF_Z), gates.reshape(bsz, seq, LANES),
                               _pad_lanes(dn_a_log[l]), _pad_lanes(dn_dt_bias[l]))
        x2 = _mix_ffn(x2, o_f.reshape(bsz * seq, DN_WIDTH), o_b.reshape(bsz * seq, DN_WIDTH), z, yb,
                      row(dn_norm_g[l]), w_out[l].astype(BF16), row(ffn_norm_g[l]),
                      w_gate[l].astype(BF16), w_up[l].astype(BF16), w_down[l].astype(BF16),
                      row(final_norm_g), final=(l == depth - 1))
    return x2.reshape(bsz, seq, d)
```

```python
import functools

import jax
import jax.numpy as jnp
from jax import lax
from jax.experimental import pallas as pl
from jax.experimental.pallas import tpu as pltpu

F32 = jnp.float32
BF16 = jnp.bfloat16

DN_HEADS = 4
DN_HEAD_DIM = 128
DN_WIDTH = DN_HEADS * DN_HEAD_DIM
SG_GROUPS = 4
SG_GROUP_DIM = 128
SG_WIDTH = SG_GROUPS * SG_GROUP_DIM
SG_CHUNK = 128
EPS = 1e-6

OFF_Z = 3 * DN_WIDTH
OFF_A = 4 * DN_WIDTH
OFF_B = OFF_A + 2 * DN_HEADS
OFF_SG = OFF_B + 2 * DN_HEADS

LANES = 128
F32_SUBLANES = 8
VMEM_LIMIT_BYTES = 56 * 1024 * 1024

DN_CHUNK = 128
NEUMANN_BLOCK = 8
DN_BATCH_ROWS = 4
TOKEN_TILE = 512
INPROJ_TILE = 1024
INPROJ_SUBTILES = 4


def _rms(x, gain):
    return x * lax.rsqrt(jnp.mean(x * x, axis=-1, keepdims=True) + EPS) * gain


def _silu(x):
    h = 0.5 * x
    return h + h * jnp.tanh(h)


def _gelu_tanh(x):
    c = 0.7978845608028654
    h = 0.5 * x
    return h + h * jnp.tanh(x * (c + (c * 0.044715) * (x * x)))


def _bdot(a, b):
    return jnp.dot(a.astype(BF16), b.astype(BF16), preferred_element_type=F32)


def _conv_silu_norm(pre, prev_row, next_row, cw_ref, qkv_ref, kt_ref, row0):
    n, width = pre.shape
    sub = F32_SUBLANES
    first = lax.broadcasted_iota(jnp.int32, (sub, width), 0) == 0
    last = lax.broadcasted_iota(jnp.int32, (sub, width), 0) == sub - 1
    x_prev = pltpu.roll(pre, 1, 0)
    x_prev = jnp.concatenate([jnp.where(first, prev_row, x_prev[:sub]), x_prev[sub:]], axis=0)
    x_next = pltpu.roll(pre, n - 1, 0)
    x_next = jnp.concatenate([x_next[:n - sub], jnp.where(last, next_row, x_next[n - sub:])], axis=0)
    qkv = _silu(cw_ref[0:1, :] * x_prev + cw_ref[1:2, :] * pre + cw_ref[2:3, :] * x_next)
    rows = slice(row0, row0 + n)
    for h in range(2 * DN_HEADS):
        cols = slice(h * DN_HEAD_DIM, (h + 1) * DN_HEAD_DIM)
        t = qkv[:, cols]
        scale = DN_HEAD_DIM ** -0.5 if h < DN_HEADS else 1.0
        t = t * (lax.rsqrt(jnp.sum(t * t, axis=-1, keepdims=True) + EPS) * scale)
        qkv_ref[rows, cols] = t.astype(BF16)
        if h >= DN_HEADS:
            kcols = slice((h - DN_HEADS) * DN_HEAD_DIM, (h - DN_HEADS + 1) * DN_HEAD_DIM)
            for c in range(n // DN_CHUNK):
                kt_ref[(row0 + c * DN_CHUNK) // DN_CHUNK, kcols, :] = (
                    t[c * DN_CHUNK:(c + 1) * DN_CHUNK, :].T.astype(BF16))
    qkv_ref[rows, 2 * DN_WIDTH:] = qkv[:, 2 * DN_WIDTH:].astype(BF16)


def _spatial_gating(ps, lng_ref, lnb_ref, ws_ref, bst_ref, og_ref, yb_ref, row0):
    ps = _gelu_tanh(ps)
    u = ps[:, :SG_WIDTH]
    v = ps[:, SG_WIDTH:]
    vc = v - jnp.mean(v, axis=-1, keepdims=True)
    v = vc * lax.rsqrt(jnp.mean(vc * vc, axis=-1, keepdims=True) + EPS) * lng_ref[...] + lnb_ref[...]
    vb = v.astype(BF16)
    for n in range(ps.shape[0] // SG_CHUNK):
        rows = slice(n * SG_CHUNK, (n + 1) * SG_CHUNK)
        for g in range(SG_GROUPS):
            cols = slice(g * SG_GROUP_DIM, (g + 1) * SG_GROUP_DIM)
            mixed = jnp.dot(ws_ref[g], vb[rows, cols], preferred_element_type=F32) + bst_ref[:, g:g + 1]
            y = _rms(u[rows, cols] * mixed, og_ref[:, cols])
            yb_ref[row0 + n * SG_CHUNK:row0 + (n + 1) * SG_CHUNK, cols] = y.astype(BF16)


def _inproj_kernel(x_ref, xprev_ref, xnext_ref, ng_ref, wqkvz_ref, wab_ref, wsg_ref, lng_ref, lnb_ref,
                   ws_ref, bst_ref, og_ref, cw_ref, qkv_ref, kt_ref, z_ref, gates_ref, yb_ref, *,
                   tiles_per_seq):
    tm = x_ref.shape[0]
    st = tm // INPROJ_SUBTILES
    halo = xprev_ref.shape[0]
    sub_rows = [slice(s * st, (s + 1) * st) for s in range(INPROJ_SUBTILES)]
    h_halo = _rms(jnp.concatenate([xprev_ref[...], xnext_ref[...]], axis=0), ng_ref[...]).astype(BF16)
    hs = [_rms(x_ref[r, :], ng_ref[...]).astype(BF16) for r in sub_rows]
    pq0 = jnp.dot(jnp.concatenate([hs[0], h_halo], axis=0), wqkvz_ref[...], preferred_element_type=F32)
    pqs = [pq0[:st]] + [jnp.dot(h, wqkvz_ref[...], preferred_element_type=F32) for h in hs[1:]]
    pss = [jnp.dot(h, wsg_ref[...], preferred_element_type=F32) for h in hs]
    gates_ref[...] = jnp.dot(jnp.concatenate(hs, axis=0), wab_ref[...],
                             preferred_element_type=F32)

    pos = pl.program_id(0) % tiles_per_seq
    tile_prev = pq0[st + halo - 1:st + halo, :OFF_Z] * (pos > 0).astype(F32)
    tile_next = pq0[st + halo:st + halo + 1, :OFF_Z] * (pos < tiles_per_seq - 1).astype(F32)
    for s, r in enumerate(sub_rows):
        z_ref[r, :] = pqs[s][:, OFF_Z:].astype(BF16)
        prev_row = tile_prev if s == 0 else pqs[s - 1][st - 1:st, :OFF_Z]
        next_row = tile_next if s == INPROJ_SUBTILES - 1 else pqs[s + 1][0:1, :OFF_Z]
        _conv_silu_norm(pqs[s][:, :OFF_Z], prev_row, next_row, cw_ref, qkv_ref, kt_ref, s * st)
    for s in range(INPROJ_SUBTILES):
        _spatial_gating(pss[s], lng_ref, lnb_ref, ws_ref, bst_ref, og_ref, yb_ref, s * st)


def _inproj(x2, ng, wqkvz, wab, wsg, lng, lnb, ws, bst, og, conv_w, seq):
    t, d = x2.shape
    tm = INPROJ_TILE
    halo = F32_SUBLANES
    per_tile = tm // halo
    n_halo = t // halo
    full = lambda *shape: pl.BlockSpec(shape, lambda i: (0,) * len(shape), pipeline_mode=pl.Buffered(1))
    rows = lambda w: pl.BlockSpec((tm, w), lambda i: (i, 0))
    prev_spec = pl.BlockSpec((halo, d), lambda i: (jnp.maximum(i * per_tile - 1, 0), 0))
    next_spec = pl.BlockSpec((halo, d), lambda i: (jnp.minimum((i + 1) * per_tile, n_halo - 1), 0))
    return pl.pallas_call(
        functools.partial(_inproj_kernel, tiles_per_seq=seq // tm),
        grid=(t // tm,),
        in_specs=[rows(d), prev_spec, next_spec, full(1, d), full(d, 4 * DN_WIDTH), full(d, LANES),
                  full(d, 2 * SG_WIDTH), full(1, SG_WIDTH), full(1, SG_WIDTH),
                  full(SG_GROUPS, SG_CHUNK, SG_CHUNK), full(SG_CHUNK, SG_GROUPS), full(1, SG_WIDTH),
                  full(3, OFF_Z)],
        out_specs=[rows(OFF_Z), pl.BlockSpec((tm // DN_CHUNK, DN_WIDTH, DN_CHUNK), lambda i: (i, 0, 0)),
                   rows(DN_WIDTH), rows(LANES), rows(SG_WIDTH)],
        out_shape=[jax.ShapeDtypeStruct((t, OFF_Z), BF16),
                   jax.ShapeDtypeStruct((t // DN_CHUNK, DN_WIDTH, DN_CHUNK), BF16),
                   jax.ShapeDtypeStruct((t, DN_WIDTH), BF16),
                   jax.ShapeDtypeStruct((t, LANES), F32), jax.ShapeDtypeStruct((t, SG_WIDTH), BF16)],
        compiler_params=pltpu.CompilerParams(dimension_semantics=("arbitrary",),
                                             vmem_limit_bytes=VMEM_LIMIT_BYTES),
        name="inproj_sg",
    )(x2, x2, x2, ng, wqkvz, wab, wsg, lng, lnb, ws, bst, og, conv_w)


def _pair(a, b):
    return jnp.concatenate([a, b], axis=1)


def _blockdiag(p):
    n = p.shape[0]
    zero = jnp.zeros((n, n), p.dtype)
    return jnp.concatenate([jnp.concatenate([p[:, :n], zero], axis=1),
                            jnp.concatenate([zero, p[:, n:]], axis=1)], axis=0)


def _pdot(p, q):
    return jnp.dot(p.astype(BF16), _blockdiag(q.astype(BF16)), preferred_element_type=F32)


def _unit_tri_inverse(a2s):
    n = a2s[0].shape[0]
    row = lax.broadcasted_iota(jnp.int32, (n, 2 * n), 0)
    colm = lax.broadcasted_iota(jnp.int32, (n, 2 * n), 1) % n
    eye = (row == colm).astype(F32)
    base = NEUMANN_BLOCK
    base_mask = (row // base) == (colm // base)
    ld = [jnp.where(base_mask, a2, 0.0) for a2 in a2s]
    ld2 = [_pdot(x, x) for x in ld]
    t = [eye - x for x in ld]
    both = [_pdot(jnp.concatenate([x, ti], axis=0), x) for x, ti in zip(ld2, t)]
    ld4 = [bo[:n] for bo in both]
    t = [ti + bo[n:] for ti, bo in zip(t, both)]
    t = [ti + _pdot(ti, x) for ti, x in zip(t, ld4)]
    m = base
    while m < n:
        joins =((row // (2 * m)) == (colm // (2 * m))) & ((row // m) != (colm // m))
        x = [_pdot(jnp.where(joins, a2, 0.0), ti) for a2, ti in zip(a2s, t)]
        t = [ti - _pdot(ti, xi) for ti, xi in zip(t, x)]
        m *= 2
    return t


def _chunk_cumsum(x, rows, reverse):
    n = x.shape[0]
    s = 1
    while s < n:
        if reverse:
            x = x + jnp.where(rows < n - s, pltpu.roll(x, n - s, 0), 0.0)
        else:
            x = x + jnp.where(rows >= s, pltpu.roll(x, s, 0), 0.0)
        s *= 2
    return x


def _delta_operands(reverse, qkv, kt_all, gates, alog_ref, dtb_ref):
    c = DN_CHUNK
    row = lax.broadcasted_iota(jnp.int32, (c, c), 0)
    col = lax.broadcasted_iota(jnp.int32, (c, c), 1)
    mxu_dtype = qkv.dtype

    z = gates + dtb_ref[...]
    softplus = jnp.maximum(z, 0.0) + jnp.log(1.0 + jnp.exp(-jnp.abs(z)))
    g = -jnp.exp(alog_ref[...]) * softplus
    beta = jax.nn.sigmoid(gates)
    gc = _chunk_cumsum(g, row, reverse)
    gct = gc.T
    g_last = gc[0:1, :] if reverse else gc[c - 1:c, :]
    g_last_t = gct[:, 0:1] if reverse else gct[:, c - 1:c]
    e_gc = jnp.exp(gc)
    e_tail_t = jnp.exp(g_last_t - gct).astype(mxu_dtype)
    e_last = jnp.exp(g_last)

    incl = (row <= col) if reverse else (row >= col)
    strict = (row < col) if reverse else (row > col)
    d = 1 if reverse else 0

    pairs = []
    for pair_idx in range(DN_HEADS // 2):
        kbs, qs, qdecs, kts, ktails, rhss, decays, e_lasts = [], [], [], [], [], [], [], []
        for h in (2 * pair_idx, 2 * pair_idx + 1):
            r = d * DN_HEADS + h
            q = qkv[:, h * DN_HEAD_DIM:(h + 1) * DN_HEAD_DIM]
            k = qkv[:, DN_WIDTH + h * DN_HEAD_DIM:DN_WIDTH + (h + 1) * DN_HEAD_DIM]
            v = qkv[:, 2 * DN_WIDTH + h * DN_HEAD_DIM:2 * DN_WIDTH + (h + 1) * DN_HEAD_DIM]
            kt = kt_all[h * DN_HEAD_DIM:(h + 1) * DN_HEAD_DIM, :]
            b_bc = jnp.broadcast_to(beta[:, 2 * DN_HEADS + r:2 * DN_HEADS + r + 1],
                                    (c, DN_HEAD_DIM)).astype(mxu_dtype)
            eg_bc = jnp.broadcast_to(e_gc[:, r:r + 1], (c, DN_HEAD_DIM)).astype(mxu_dtype)
            kb = k * b_bc
            decays.append(jnp.exp(jnp.where(incl, gc[:, r:r + 1] - gct[r:r + 1, :], -1e30)))
            kbs.append(kb)
            qs.append(q)
            qdecs.append(q * eg_bc)
            kts.append(kt)
            ktails.append(kt * e_tail_t[r:r + 1, :])
            rhss.append(jnp.concatenate([v * b_bc, kb * eg_bc], axis=1))
            e_lasts.append(jnp.broadcast_to(e_last[:, r:r + 1], (1, DN_HEAD_DIM)))
        gram = _pdot(jnp.concatenate([_pair(*kbs), _pair(*qs)], axis=0), _pair(*kts))
        pairs.append(dict(
            reverse=reverse, gram=gram, decay2=_pair(*decays), strict2=_pair(strict, strict),
            rhss=rhss, qdec2=_pair(*qdecs), ktail2=_pair(*ktails), e_last2=_pair(*e_lasts)))
    return pairs


def _wy_solve(a_list, rhs_list, rev_list):
    c = DN_CHUNK
    hc = c // 2
    lane = lax.broadcasted_iota(jnp.int32, (hc, c), 1)
    diag = [jnp.where(lane < hc, a[:hc], a[hc:]) for a in a_list]
    t12 = _unit_tri_inverse(diag)
    t21 = [pltpu.roll(t, hc, 1) for t in t12]
    t_first = [(t21 if rev else t12)[i][:, :hc] for i, rev in enumerate(rev_list)]
    t_second = [(t12 if rev else t21)[i][:, :hc] for i, rev in enumerate(rev_list)]
    a_off = [pltpu.roll(a[:hc], hc, 1)[:, :hc] if rev else a[hc:, :hc] for a, rev in zip(a_list, rev_list)]
    r_first = [r_[hc:] if rev else r_[:hc] for r_, rev in zip(rhs_list, rev_list)]
    r_second = [r_[:hc] if rev else r_[hc:] for r_, rev in zip(rhs_list, rev_list)]
    x_first = [_bdot(t, r_) for t, r_ in zip(t_first, r_first)]
    y = [_bdot(a, x) for a, x in zip(a_off, x_first)]
    x_second = [_bdot(t, r_ - y_) for t, r_, y_ in zip(t_second, r_second, y)]
    return [jnp.concatenate([x2, x1] if rev else [x1, x2], axis=0)
            for x1, x2, rev in zip(x_first, x_second, rev_list)]


def _delta_chunks(pairs, states):
    c = DN_CHUNK
    grams = [p["gram"] for p in pairs]
    a2s =[jnp.where(p["strict2"], gm[:c] * p["decay2"], 0.0) for p, gm in zip(pairs, grams)]
    attn2s = [gm[c:] * p["decay2"] for p, gm in zip(pairs, grams)]
    uw = _wy_solve([a2[:, i * c:(i + 1) * c] for a2 in a2s for i in range(2)],
                   [p["rhss"][i] for p in pairs for i in range(2)],
                   [p["reverse"] for p in pairs for i in range(2)])
    u2s = [_pair(uw[2 * j][:, :DN_HEAD_DIM], uw[2 * j + 1][:, :DN_HEAD_DIM]) for j in range(len(pairs))]
    w2s = [_pair(uw[2 * j][:, DN_HEAD_DIM:], uw[2 * j + 1][:, DN_HEAD_DIM:]) for j in range(len(pairs))]
    rss = [_pdot(jnp.concatenate([w2.astype(BF16), p["qdec2"].astype(BF16)], axis=0), s2)
           for p, w2, s2 in zip(pairs, w2s, states)]
    v_news = [u2 - rs[:c] for u2, rs in zip(u2s, rss)]
    rvs = [_pdot(jnp.concatenate([attn2.astype(BF16), p["ktail2"].astype(BF16)], axis=0), vn)
           for p, attn2, vn in zip(pairs, attn2s, v_news)]
    outs = [rs[c:] + rv[:c] for rs, rv in zip(rss, rvs)]
    new_states = [s2 * p["e_last2"] + rv[c:] for p, s2, rv in zip(pairs, states, rvs)]
    return outs, new_states


def _delta_kernel(xf_ref, xb_ref, ktf_ref, ktb_ref, gf_ref, gb_ref, alog_ref, dtb_ref, of_ref, ob_ref, s_ref):
    n_rows = xf_ref.shape[0]
    half = DN_HEADS // 2

    @pl.when(pl.program_id(1) == 0)
    def _():
        s_ref[...] = jnp.zeros_like(s_ref)

    pairs = []
    for b in range(n_rows):
        pairs += _delta_operands(False, xf_ref[b], ktf_ref[b, 0], gf_ref[b], alog_ref, dtb_ref)
        pairs += _delta_operands(True, xb_ref[b], ktb_ref[b, 0], gb_ref[b], alog_ref, dtb_ref)
    outs, new_states = _delta_chunks(pairs, [s_ref[i] for i in range(len(pairs))])
    for i, (o, s2) in enumerate(zip(outs, new_states)):
        b, d, pair_idx = i // (2 * half), (i // half) % 2, i % half
        o_ref = ob_ref if d else of_ref
        o_ref[b, :, 2 * pair_idx * DN_HEAD_DIM:(2 * pair_idx + 2) * DN_HEAD_DIM] = o
        s_ref[i] = s2


def _delta_rule(qkv, kt, gates, alog_row, dtb_row):
    bsz, seq, width = qkv.shape
    c = DN_CHUNK
    nb = DN_BATCH_ROWS
    nc = seq // c
    fwd = lambda n: n
    bwd = lambda n: nc - 1 - n
    chunk_spec = lambda w, chunk_of: pl.BlockSpec((nb, c, w), lambda b, n: (b, chunk_of(n), 0))
    kt_spec = lambda chunk_of: pl.BlockSpec((nb, 1, DN_WIDTH, c), lambda b, n: (b, chunk_of(n), 0, 0))
    full = lambda *shape: pl.BlockSpec(shape, lambda b, n: (0,) * len(shape))
    out_sds = jax.ShapeDtypeStruct((bsz, seq, DN_WIDTH), F32)
    return pl.pallas_call(
        _delta_kernel,
        grid=(bsz // nb, nc),
        in_specs=[chunk_spec(width, fwd), chunk_spec(width, bwd), kt_spec(fwd), kt_spec(bwd),
                  chunk_spec(LANES, fwd), chunk_spec(LANES, bwd), full(1, LANES), full(1, LANES)],
        out_specs=[chunk_spec(DN_WIDTH, fwd), chunk_spec(DN_WIDTH, bwd)],
        out_shape=[out_sds, out_sds],
        scratch_shapes=[pltpu.VMEM((nb * DN_HEADS, DN_HEAD_DIM, 2 * DN_HEAD_DIM), F32)],
        compiler_params=pltpu.CompilerParams(dimension_semantics=("arbitrary", "arbitrary"),
                                             vmem_limit_bytes=VMEM_LIMIT_BYTES),
        name="delta_rule",
    )(qkv, qkv, kt, kt, gates, gates, alog_row, dtb_row)


def _mix_ffn_kernel(x_ref, of_ref, ob_ref, z_ref, yb_ref, dng_ref, wo_ref, fg_ref, wg_ref, wu_ref, wd_ref,
                    fin_ref, out_ref, *, final):
    o = of_ref[...] + ob_ref[...]
    z = z_ref[...].astype(F32)
    heads = []
    for h in range(DN_HEADS):
        cols = slice(h * DN_HEAD_DIM, (h + 1) * DN_HEAD_DIM)
        heads.append((_rms(o[:, cols], dng_ref[...]) * _silu(z[:, cols])).astype(BF16))
    ya = jnp.concatenate(heads, axis=1)
    x1 = (x_ref[...] + jnp.dot(ya, wo_ref[:DN_WIDTH, :], preferred_element_type=F32)
          + jnp.dot(yb_ref[...], wo_ref[DN_WIDTH:, :], preferred_element_type=F32))
    hb = _rms(x1, fg_ref[...]).astype(BF16)
    gate = jnp.dot(hb, wg_ref[...], preferred_element_type=F32)
    up = jnp.dot(hb, wu_ref[...], preferred_element_type=F32)
    hid = (_silu(gate) * up).astype(BF16)
    out = x1 + jnp.dot(hid, wd_ref[...], preferred_element_type=F32)
    if final:
        out = _rms(out, fin_ref[...])
    out_ref[...] = out


def _mix_ffn(x2, o_f, o_b, z, yb, dng, wo, fg, wg, wu, wd, fin, final):
    t, d = x2.shape
    f = wg.shape[1]
    tm = TOKEN_TILE
    once = pl.Buffered(1)
    full = lambda *shape: pl.BlockSpec(shape, lambda i: (0,) * len(shape), pipeline_mode=once)
    rows = lambda w: pl.BlockSpec((tm, w), lambda i: (i, 0))
    return pl.pallas_call(
        functools.partial(_mix_ffn_kernel, final=final),
        grid=(t // tm,),
        in_specs=[rows(d), rows(DN_WIDTH), rows(DN_WIDTH), rows(DN_WIDTH), rows(SG_WIDTH),
                  full(1, DN_HEAD_DIM), full(DN_WIDTH + SG_WIDTH, d), full(1, d), full(d, f), full(d, f),
                  full(f, d), full(1, d)],
        out_specs=rows(d),
        out_shape=jax.ShapeDtypeStruct((t, d), F32),
        compiler_params=pltpu.CompilerParams(dimension_semantics=("arbitrary",),
                                             vmem_limit_bytes=VMEM_LIMIT_BYTES),
        name="mix_ffn",
    )(x2, o_f, o_b, z, yb, dng, wo, fg, wg, wu, wd, fin)


def _pad_lanes(v):
    flat = v.reshape(1, -1).astype(F32)
    return jnp.pad(flat, ((0, 0), (0, LANES - flat.shape[1])))


def kernel(x, mix_norm_g, w_in, conv_w, dn_a_log, dn_dt_bias, dn_norm_g, sg_ln_g, sg_ln_b, sg_w, sg_b,
           sg_out_g, w_out, ffn_norm_g, w_gate, w_up, w_down, final_norm_g):
    bsz, seq, d = x.shape
    depth = w_in.shape[0]
    assert seq % TOKEN_TILE == 0 and seq % INPROJ_TILE == 0 and seq % DN_CHUNK == 0
    assert (INPROJ_TILE // INPROJ_SUBTILES) % SG_CHUNK == 0
    assert bsz % DN_BATCH_ROWS == 0
    x2 = x.reshape(bsz * seq, d)
    row = lambda v: v.reshape(1, -1).astype(F32)
    for l in range(depth):
        wl = w_in[l]
        wqkvz = wl[:, :OFF_A].astype(BF16)
        wab = jnp.pad(wl[:, OFF_A:OFF_SG], ((0, 0), (0, LANES - (OFF_SG - OFF_A)))).astype(BF16)
        wsg = wl[:, OFF_SG:].astype(BF16)
        qkv, kt, z, gates, yb = _inproj(x2, row(mix_norm_g[l]), wqkvz, wab, wsg, row(sg_ln_g[l]),
                                    row(sg_ln_b[l]), sg_w[l].astype(BF16), sg_b[l].T.astype(F32),
                                    row(sg_out_g[l]), conv_w[l].astype(F32), seq)
        o_f, o_b = _delta_rule(qkv.reshape(bsz, seq, OFF_Z),
                               kt.reshape(bsz, seq // DN_CHUNK, DN_WIDTH, DN_CHUNK),
                               gates.reshape(bsz, seq, LANES),
                               _pad_lanes(dn_a_log[l]), _pad_lanes(dn_dt_bias[l]))
        x2 = _mix_ffn(x2, o_f.reshape(bsz * seq, DN_WIDTH), o_b.reshape(bsz * seq, DN_WIDTH), z, yb,
                      row(dn_norm_g[l]), w_out[l].astype(BF16), row(ffn_norm_g[l]),
                      w_gate[l].astype(BF16), w_up[l].astype(BF16), w_down[l].astype(BF16),
                      row(final_norm_g), final=(l == depth - 1))
    return x2.reshape(bsz, seq, d)
```

```python
import functools

import jax
import jax.numpy as jnp
from jax import lax
from jax.experimental import pallas as pl
from jax.experimental.pallas import tpu as pltpu

F32 = jnp.float32
BF16 = jnp.bfloat16

DN_HEADS = 4
DN_HEAD_DIM = 128
DN_WIDTH = DN_HEADS * DN_HEAD_DIM
SG_GROUPS = 4
SG_GROUP_DIM = 128
SG_WIDTH = SG_GROUPS * SG_GROUP_DIM
SG_CHUNK = 128
EPS = 1e-6

OFF_Z = 3 * DN_WIDTH
OFF_A = 4 * DN_WIDTH
OFF_B = OFF_A + 2 * DN_HEADS
OFF_SG = OFF_B + 2 * DN_HEADS

LANES = 128
F32_SUBLANES = 8
VMEM_LIMIT_BYTES = 56 * 1024 * 1024

DN_CHUNK = 128
NEUMANN_BLOCK = 8
DN_BATCH_ROWS = 4
TOKEN_TILE = 512
INPROJ_TILE = 1024
INPROJ_SUBTILES = 4


def _rms(x, gain):
    return x * lax.rsqrt(jnp.mean(x * x, axis=-1, keepdims=True) + EPS) * gain


def _silu(x):
    h = 0.5 * x
    return h + h * jnp.tanh(h)


def _gelu_tanh(x):
    c = 0.7978845608028654
    h = 0.5 * x
    return h + h * jnp.tanh(x * (c + (c * 0.044715) * (x * x)))


def _bdot(a, b):
    return jnp.dot(a.astype(BF16), b.astype(BF16), preferred_element_type=F32)


def _conv_silu_norm(pre, prev_row, next_row, cw_ref, qkv_ref, kt_ref, row0):
    n, width = pre.shape
    sub = F32_SUBLANES
    first = lax.broadcasted_iota(jnp.int32, (sub, width), 0) == 0
    last = lax.broadcasted_iota(jnp.int32, (sub, width), 0) == sub - 1
    x_prev = pltpu.roll(pre, 1, 0)
    x_prev = jnp.concatenate([jnp.where(first, prev_row, x_prev[:sub]), x_prev[sub:]], axis=0)
    x_next = pltpu.roll(pre, n - 1, 0)
    x_next = jnp.concatenate([x_next[:n - sub], jnp.where(last, next_row, x_next[n - sub:])], axis=0)
    qkv = _silu(cw_ref[0:1, :] * x_prev + cw_ref[1:2, :] * pre + cw_ref[2:3, :] * x_next)
    rows = slice(row0, row0 + n)
    for h in range(2 * DN_HEADS):
        cols = slice(h * DN_HEAD_DIM, (h + 1) * DN_HEAD_DIM)
        t = qkv[:, cols]
        scale = DN_HEAD_DIM ** -0.5 if h < DN_HEADS else 1.0
        t = t * (lax.rsqrt(jnp.sum(t * t, axis=-1, keepdims=True) + EPS) * scale)
        qkv_ref[rows, cols] = t.astype(BF16)
        if h >= DN_HEADS:
            kcols = slice((h - DN_HEADS) * DN_HEAD_DIM, (h - DN_HEADS + 1) * DN_HEAD_DIM)
            for c in range(n // DN_CHUNK):
                kt_ref[(row0 + c * DN_CHUNK) // DN_CHUNK, kcols, :] = (
                    t[c * DN_CHUNK:(c + 1) * DN_CHUNK, :].T.astype(BF16))
    qkv_ref[rows, 2 * DN_WIDTH:] = qkv[:, 2 * DN_WIDTH:].astype(BF16)


def _spatial_gating(ps, lng_ref, lnb_ref, ws_ref, bst_ref, og_ref, yb_ref, row0):
    ps = _gelu_tanh(ps)
    u = ps[:, :SG_WIDTH]
    v = ps[:, SG_WIDTH:]
    vc = v - jnp.mean(v, axis=-1, keepdims=True)
    v = vc * lax.rsqrt(jnp.mean(vc * vc, axis=-1, keepdims=True) + EPS) * lng_ref[...] + lnb_ref[...]
    vb = v.astype(BF16)
    for n in range(ps.shape[0] // SG_CHUNK):
        rows = slice(n * SG_CHUNK, (n + 1) * SG_CHUNK)
        for g in range(SG_GROUPS):
            cols = slice(g * SG_GROUP_DIM, (g + 1) * SG_GROUP_DIM)
            mixed = jnp.dot(ws_ref[g], vb[rows, cols], preferred_element_type=F32) + bst_ref[:, g:g + 1]
            y = _rms(u[rows, cols] * mixed, og_ref[:, cols])
            yb_ref[row0 + n * SG_CHUNK:row0 + (n + 1) * SG_CHUNK, cols] = y.astype(BF16)


def _inproj_kernel(x_ref, xprev_ref, xnext_ref, ng_ref, wqkvz_ref, wab_ref, wsg_ref, lng_ref, lnb_ref,
                   ws_ref, bst_ref, og_ref, cw_ref, qkv_ref, kt_ref, z_ref, gates_ref, yb_ref, *,
                   tiles_per_seq):
    tm = x_ref.shape[0]
    st = tm // INPROJ_SUBTILES
    halo = xprev_ref.shape[0]
    sub_rows = [slice(s * st, (s + 1) * st) for s in range(INPROJ_SUBTILES)]
    h_halo = _rms(jnp.concatenate([xprev_ref[...], xnext_ref[...]], axis=0), ng_ref[...]).astype(BF16)
    hs = [_rms(x_ref[r, :], ng_ref[...]).astype(BF16) for r in sub_rows]
    pq0 = jnp.dot(jnp.concatenate([hs[0], h_halo], axis=0), wqkvz_ref[...], preferred_element_type=F32)
    pqs = [pq0[:st]] + [jnp.dot(h, wqkvz_ref[...], preferred_element_type=F32) for h in hs[1:]]
    pss = [jnp.dot(h, wsg_ref[...], preferred_element_type=F32) for h in hs]
    gates_ref[...] = jnp.dot(jnp.concatenate(hs, axis=0), wab_ref[...],
                             preferred_element_type=F32)

    pos = pl.program_id(0) % tiles_per_seq
    tile_prev = pq0[st + halo - 1:st + halo, :OFF_Z] * (pos > 0).astype(F32)
    tile_next = pq0[st + halo:st + halo + 1, :OFF_Z] * (pos < tiles_per_seq - 1).astype(F32)
    for s, r in enumerate(sub_rows):
        z_ref[r, :] = pqs[s][:, OFF_Z:].astype(BF16)
        prev_row = tile_prev if s == 0 else pqs[s - 1][st - 1:st, :OFF_Z]
        next_row = tile_next if s == INPROJ_SUBTILES - 1 else pqs[s + 1][0:1, :OFF_Z]
        _conv_silu_norm(pqs[s][:, :OFF_Z], prev_row, next_row, cw_ref, qkv_ref, kt_ref, s * st)
    for s in range(INPROJ_SUBTILES):
        _spatial_gating(pss[s], lng_ref, lnb_ref, ws_ref, bst_ref, og_ref, yb_ref, s * st)


def _inproj(x2, ng, wqkvz, wab, wsg, lng, lnb, ws, bst, og, conv_w, seq):
    t, d = x2.shape
    tm = INPROJ_TILE
    halo = F32_SUBLANES
    per_tile = tm // halo
    n_halo = t // halo
    full = lambda *shape: pl.BlockSpec(shape, lambda i: (0,) * len(shape), pipeline_mode=pl.Buffered(1))
    rows = lambda w: pl.BlockSpec((tm, w), lambda i: (i, 0))
    prev_spec = pl.BlockSpec((halo, d), lambda i: (jnp.maximum(i * per_tile - 1, 0), 0))
    next_spec = pl.BlockSpec((halo, d), lambda i: (jnp.minimum((i + 1) * per_tile, n_halo - 1), 0))
    return pl.pallas_call(
        functools.partial(_inproj_kernel, tiles_per_seq=seq // tm),
        grid=(t // tm,),
        in_specs=[rows(d), prev_spec, next_spec, full(1, d), full(d, 4 * DN_WIDTH), full(d, LANES),
                  full(d, 2 * SG_WIDTH), full(1, SG_WIDTH), full(1, SG_WIDTH),
                  full(SG_GROUPS, SG_CHUNK, SG_CHUNK), full(SG_CHUNK, SG_GROUPS), full(1, SG_WIDTH),
                  full(3, OFF_Z)],
        out_specs=[rows(OFF_Z), pl.BlockSpec((tm // DN_CHUNK, DN_WIDTH, DN_CHUNK), lambda i: (i, 0, 0)),
                   rows(DN_WIDTH), rows(LANES), rows(SG_WIDTH)],
        out_shape=[jax.ShapeDtypeStruct((t, OFF_Z), BF16),
                   jax.ShapeDtypeStruct((t // DN_CHUNK, DN_WIDTH, DN_CHUNK), BF16),
                   jax.ShapeDtypeStruct((t, DN_WIDTH), BF16),
                   jax.ShapeDtypeStruct((t, LANES), F32), jax.ShapeDtypeStruct((t, SG_WIDTH), BF16)],
        compiler_params=pltpu.CompilerParams(dimension_semantics=("arbitrary",),
                                             vmem_limit_bytes=VMEM_LIMIT_BYTES),
        name="inproj_sg",
    )(x2, x2, x2, ng, wqkvz, wab, wsg, lng, lnb, ws, bst, og, conv_w)


def _blockdiag(p):
    n = p.shape[0]
    zero = jnp.zeros((n, n), p.dtype)
    return jnp.concatenate([jnp.concatenate([p[:, :n], zero], axis=1),
                            jnp.concatenate([zero, p[:, n:]], axis=1)], axis=0)


def _pdot(p, q):
    return jnp.dot(p.astype(BF16), _blockdiag(q.astype(BF16)), preferred_element_type=F32)


def _unit_tri_inverse(a2s):
    n = a2s[0].shape[0]
    row = lax.broadcasted_iota(jnp.int32, (n, 2 * n), 0)
    colm = lax.broadcasted_iota(jnp.int32, (n, 2 * n), 1) % n
    eye = (row == colm).astype(F32)
    base = NEUMANN_BLOCK
    base_mask = (row // base) == (colm // base)
    ld = [jnp.where(base_mask, a2, 0.0) for a2 in a2s]
    ld2 = [_pdot(x, x) for x in ld]
    t = [eye - x for x in ld]
    both = [_pdot(jnp.concatenate([x, ti], axis=0), x) for x, ti in zip(ld2, t)]
    ld4 = [bo[:n] for bo in both]
    t = [ti + bo[n:] for ti, bo in zip(t, both)]
    t = [ti + _pdot(ti, x) for ti, x in zip(t, ld4)]

    levels = []
    m = base
    while m < n:
        levels.append(((row // (2 * m)) == (colm // (2 * m))) & ((row // m) != (colm // m)))
        m *= 2
    if not levels:
        return t
    a_t = [_pdot(jnp.concatenate([jnp.where(joins, a2, 0.0) for joins in levels], axis=0), ti)
           for a2, ti in zip(a2s, t)]
    w = [[at[j * n:(j + 1) * n] for j in range(len(levels))] for at in a_t]
    for lvl in range(len(levels)):
        x = [wi[lvl] for wi in w]
        prod = [_pdot(jnp.concatenate([ti] + wi[lvl + 1:], axis=0), xi) for ti, wi, xi in zip(t, w, x)]
        t = [ti - pr[:n] for ti, pr in zip(t, prod)]
        w = [wi[:lvl + 1] + [wj - pr[(j + 1) * n:(j + 2) * n] for j, wj in enumerate(wi[lvl + 1:])]
             for wi, pr in zip(w, prod)]
    return t


def _chunk_cumsum(x, rows, reverse):
    n = x.shape[0]
    s = 1
    while s < n:
        if reverse:
            x = x + jnp.where(rows < n - s, pltpu.roll(x, n - s, 0), 0.0)
        else:
            x = x + jnp.where(rows >= s, pltpu.roll(x, s, 0), 0.0)
        s *= 2
    return x


def _delta_operands(reverse, qkv, kt_all, gates, alog_ref, dtb_ref):
    c = DN_CHUNK
    row = lax.broadcasted_iota(jnp.int32, (c, c), 0)
    col = lax.broadcasted_iota(jnp.int32, (c, c), 1)
    mxu_dtype = qkv.dtype

    z = gates + dtb_ref[...]
    softplus = jnp.maximum(z, 0.0) + jnp.log(1.0 + jnp.exp(-jnp.abs(z)))
    g = -jnp.exp(alog_ref[...]) * softplus
    beta = jax.nn.sigmoid(gates)
    gc = _chunk_cumsum(g, row, reverse)
    gct = gc.T
    g_last = gc[0:1, :] if reverse else gc[c - 1:c, :]
    g_last_t = gct[:, 0:1] if reverse else gct[:, c - 1:c]
    e_gc = jnp.exp(gc)
    e_tail_t = jnp.exp(g_last_t - gct).astype(mxu_dtype)
    e_last = jnp.exp(g_last)

    incl = (row <= col) if reverse else (row >= col)
    strict = (row < col) if reverse else (row > col)
    d = 1 if reverse else 0

    chains = []
    for h in range(DN_HEADS):
        r = d * DN_HEADS + h
        q = qkv[:, h * DN_HEAD_DIM:(h + 1) * DN_HEAD_DIM]
        k = qkv[:, DN_WIDTH + h * DN_HEAD_DIM:DN_WIDTH + (h + 1) * DN_HEAD_DIM]
        v = qkv[:, 2 * DN_WIDTH + h * DN_HEAD_DIM:2 * DN_WIDTH + (h + 1) * DN_HEAD_DIM]
        kt = kt_all[h * DN_HEAD_DIM:(h + 1) * DN_HEAD_DIM, :]
        b_bc = jnp.broadcast_to(beta[:, 2 * DN_HEADS + r:2 * DN_HEADS + r + 1],
                                (c, DN_HEAD_DIM)).astype(mxu_dtype)
        eg_bc = jnp.broadcast_to(e_gc[:, r:r + 1], (c, DN_HEAD_DIM)).astype(mxu_dtype)
        kb = k * b_bc
        chains.append(dict(
            reverse=reverse, strict=strict, kt=kt,
            gram_lhs=jnp.concatenate([kb, q], axis=0),
            decay=jnp.exp(jnp.where(incl, gc[:, r:r + 1] - gct[r:r + 1, :], -1e30)),
            rhs=jnp.concatenate([v * b_bc, kb * eg_bc], axis=1),
            qdec=q * eg_bc, ktail=kt * e_tail_t[r:r + 1, :],
            e_last=jnp.broadcast_to(e_last[:, r:r + 1], (1, DN_HEAD_DIM))))
    return chains


def _wy_solve(a_list, rhs_list, rev_list):
    c = DN_CHUNK
    hc = c // 2
    lane = lax.broadcasted_iota(jnp.int32, (hc, c), 1)
    diag = [jnp.where(lane < hc, a[:hc], a[hc:]) for a in a_list]
    t12 = _unit_tri_inverse(diag)
    t21 = [pltpu.roll(t, hc, 1) for t in t12]
    t_first = [(t21 if rev else t12)[i][:, :hc] for i, rev in enumerate(rev_list)]
    t_second = [(t12 if rev else t21)[i][:, :hc] for i, rev in enumerate(rev_list)]
    a_off = [pltpu.roll(a[:hc], hc, 1)[:, :hc] if rev else a[hc:, :hc] for a, rev in zip(a_list, rev_list)]
    r_first = [r_[hc:] if rev else r_[:hc] for r_, rev in zip(rhs_list, rev_list)]
    r_second = [r_[:hc] if rev else r_[hc:] for r_, rev in zip(rhs_list, rev_list)]
    x_first = [_bdot(t, r_) for t, r_ in zip(t_first, r_first)]
    y = [_bdot(a, x) for a, x in zip(a_off, x_first)]
    x_second = [_bdot(t, r_ - y_) for t, r_, y_ in zip(t_second, r_second, y)]
    return [jnp.concatenate([x2, x1] if rev else [x1, x2], axis=0)
            for x1, x2, rev in zip(x_first, x_second, rev_list)]


def _delta_chunks(chains, states):
    c = DN_CHUNK
    grams = [_bdot(p["gram_lhs"], p["kt"]) for p in chains]
    a_s = [jnp.where(p["strict"], gm[:c] * p["decay"], 0.0) for p, gm in zip(chains, grams)]
    attns = [gm[c:] * p["decay"] for p, gm in zip(chains, grams)]
    uw = _wy_solve(a_s, [p["rhs"] for p in chains], [p["reverse"] for p in chains])
    rss = [_bdot(jnp.concatenate([x[:, DN_HEAD_DIM:].astype(BF16), p["qdec"].astype(BF16)], axis=0), s_)
           for p, x, s_ in zip(chains, uw, states)]
    v_news = [x[:, :DN_HEAD_DIM] - rs[:c] for x, rs in zip(uw, rss)]
    rvs = [_bdot(jnp.concatenate([at.astype(BF16), p["ktail"].astype(BF16)], axis=0), vn)
           for p, at, vn in zip(chains, attns, v_news)]
    outs = [rs[c:] + rv[:c] for rs, rv in zip(rss, rvs)]
    new_states = [s_ * p["e_last"] + rv[c:] for p, s_, rv in zip(chains, states, rvs)]
    return outs, new_states


def _delta_kernel(xf_ref, xb_ref, ktf_ref, ktb_ref, gf_ref, gb_ref, alog_ref, dtb_ref, of_ref, ob_ref, s_ref):
    n_rows = xf_ref.shape[0]

    @pl.when(pl.program_id(1) == 0)
    def _():
        s_ref[...] = jnp.zeros_like(s_ref)

    chains = []
    for b in range(n_rows):
        chains += _delta_operands(False, xf_ref[b], ktf_ref[b, 0], gf_ref[b], alog_ref, dtb_ref)
        chains += _delta_operands(True, xb_ref[b], ktb_ref[b, 0], gb_ref[b], alog_ref, dtb_ref)
    outs, new_states = _delta_chunks(chains, [s_ref[i] for i in range(len(chains))])
    for i, (o, s_) in enumerate(zip(outs, new_states)):
        b, h = i // (2 * DN_HEADS), i % DN_HEADS
        o_ref = ob_ref if chains[i]["reverse"] else of_ref
        o_ref[b, :, h * DN_HEAD_DIM:(h + 1) * DN_HEAD_DIM] = o
        s_ref[i] = s_


def _delta_rule(qkv, kt, gates, alog_row, dtb_row):
    bsz, seq, width = qkv.shape
    c = DN_CHUNK
    nb = DN_BATCH_ROWS
    nc = seq // c
    fwd = lambda n: n
    bwd = lambda n: nc - 1 - n
    chunk_spec = lambda w, chunk_of: pl.BlockSpec((nb, c, w), lambda b, n: (b, chunk_of(n), 0))
    kt_spec = lambda chunk_of: pl.BlockSpec((nb, 1, DN_WIDTH, c), lambda b, n: (b, chunk_of(n), 0, 0))
    full = lambda *shape: pl.BlockSpec(shape, lambda b, n: (0,) * len(shape))
    out_sds = jax.ShapeDtypeStruct((bsz, seq, DN_WIDTH), F32)
    return pl.pallas_call(
        _delta_kernel,
        grid=(bsz // nb, nc),
        in_specs=[chunk_spec(width, fwd), chunk_spec(width, bwd), kt_spec(fwd), kt_spec(bwd),
                  chunk_spec(LANES, fwd), chunk_spec(LANES, bwd), full(1, LANES), full(1, LANES)],
        out_specs=[chunk_spec(DN_WIDTH, fwd), chunk_spec(DN_WIDTH, bwd)],
        out_shape=[out_sds, out_sds],
        scratch_shapes=[pltpu.VMEM((2 * nb * DN_HEADS, DN_HEAD_DIM, DN_HEAD_DIM), F32)],
        compiler_params=pltpu.CompilerParams(dimension_semantics=("arbitrary", "arbitrary"),
                                             vmem_limit_bytes=VMEM_LIMIT_BYTES),
        name="delta_rule",
    )(qkv, qkv, kt, kt, gates, gates, alog_row, dtb_row)


def _mix_ffn_kernel(x_ref, of_ref, ob_ref, z_ref, yb_ref, dng_ref, wo_ref, fg_ref, wg_ref, wu_ref, wd_ref,
                    fin_ref, out_ref, *, final):
    x1 = x_ref[...] + jnp.dot(yb_ref[...], wo_ref[DN_WIDTH:, :], preferred_element_type=F32)
    o = of_ref[...] + ob_ref[...]
    z = z_ref[...].astype(F32)
    heads = []
    for h in range(DN_HEADS):
        cols = slice(h * DN_HEAD_DIM, (h + 1) * DN_HEAD_DIM)
        heads.append((_rms(o[:, cols], dng_ref[...]) * _silu(z[:, cols])).astype(BF16))
    x1 = x1 + jnp.dot(jnp.concatenate(heads, axis=1), wo_ref[:DN_WIDTH, :], preferred_element_type=F32)
    hb = _rms(x1, fg_ref[...]).astype(BF16)
    gate = jnp.dot(hb, wg_ref[...], preferred_element_type=F32)
    up = jnp.dot(hb, wu_ref[...], preferred_element_type=F32)
    hid = (_silu(gate) * up).astype(BF16)
    out = x1 + jnp.dot(hid, wd_ref[...], preferred_element_type=F32)
    if final:
        out = _rms(out, fin_ref[...])
    out_ref[...] = out


def _mix_ffn(x2, o_f, o_b, z, yb, dng, wo, fg, wg, wu, wd, fin, final):
    t, d = x2.shape
    f = wg.shape[1]
    tm = TOKEN_TILE
    once = pl.Buffered(1)
    full = lambda *shape: pl.BlockSpec(shape, lambda i: (0,) * len(shape), pipeline_mode=once)
    rows = lambda w: pl.BlockSpec((tm, w), lambda i: (i, 0))
    return pl.pallas_call(
        functools.partial(_mix_ffn_kernel, final=final),
        grid=(t // tm,),
        in_specs=[rows(d), rows(DN_WIDTH), rows(DN_WIDTH), rows(DN_WIDTH), rows(SG_WIDTH),
                  full(1, DN_HEAD_DIM), full(DN_WIDTH + SG_WIDTH, d), full(1, d), full(d, f), full(d, f),
                  full(f, d), full(1, d)],
        out_specs=rows(d),
        out_shape=jax.ShapeDtypeStruct((t, d), F32),
        compiler_params=pltpu.CompilerParams(dimension_semantics=("arbitrary",),
                                             vmem_limit_bytes=VMEM_LIMIT_BYTES),
        name="mix_ffn",
    )(x2, o_f, o_b, z, yb, dng, wo, fg, wg, wu, wd, fin)


def _pad_lanes(v):
    flat = v.reshape(1, -1).astype(F32)
    return jnp.pad(flat, ((0, 0), (0, LANES - flat.shape[1])))


def kernel(x, mix_norm_g, w_in, conv_w, dn_a_log, dn_dt_bias, dn_norm_g, sg_ln_g, sg_ln_b, sg_w, sg_b,
           sg_out_g, w_out, ffn_norm_g, w_gate, w_up, w_down, final_norm_g):
    bsz, seq, d = x.shape
    depth = w_in.shape[0]
    assert seq % TOKEN_TILE == 0 and seq % INPROJ_TILE == 0 and seq % DN_CHUNK == 0
    assert (INPROJ_TILE // INPROJ_SUBTILES) % SG_CHUNK == 0
    assert bsz % DN_BATCH_ROWS == 0
    x2 = x.reshape(bsz * seq, d)
    row = lambda v: v.reshape(1, -1).astype(F32)
    for l in range(depth):
        wl = w_in[l]
        wqkvz = wl[:, :OFF_A].astype(BF16)
        wab = jnp.pad(wl[:, OFF_A:OFF_SG], ((0, 0), (0, LANES - (OFF_SG - OFF_A)))).astype(BF16)
        wsg = wl[:, OFF_SG:].astype(BF16)
        qkv, kt, z, gates, yb = _inproj(x2, row(mix_norm_g[l]), wqkvz, wab, wsg, row(sg_ln_g[l]),
                                    row(sg_ln_b[l]), sg_w[l].astype(BF16), sg_b[l].T.astype(F32),
                                    row(sg_out_g[l]), conv_w[l].astype(F32), seq)
        o_f, o_b = _delta_rule(qkv.reshape(bsz, seq, OFF_Z),
                               kt.reshape(bsz, seq // DN_CHUNK, DN_WIDTH, DN_CHUNK),
                               gates.reshape(bsz, seq, LANES),
                               _pad_lanes(dn_a_log[l]), _pad_lanes(dn_dt_bias[l]))
        x2 = _mix_ffn(x2, o_f.reshape(bsz * seq, DN_WIDTH), o_b.reshape(bsz * seq, DN_WIDTH), z, yb,
                      row(dn_norm_g[l]), w_out[l].astype(BF16), row(ffn_norm_g[l]),
                      w_gate[l].astype(BF16), w_up[l].astype(BF16), w_down[l].astype(BF16),
                      row(final_norm_g), final=(l == depth - 1))
    return x2.reshape(bsz, seq, d)
```

```python
import functools

import jax
import jax.numpy as jnp
from jax import lax
from jax.experimental import pallas as pl
from jax.experimental.pallas import tpu as pltpu

F32 = jnp.float32
BF16 = jnp.bfloat16

DN_HEADS = 4
DN_HEAD_DIM = 128
DN_WIDTH = DN_HEADS * DN_HEAD_DIM
SG_GROUPS = 4
SG_GROUP_DIM = 128
SG_WIDTH = SG_GROUPS * SG_GROUP_DIM
SG_CHUNK = 128
EPS = 1e-6

OFF_Z = 3 * DN_WIDTH
OFF_A = 4 * DN_WIDTH
OFF_B = OFF_A + 2 * DN_HEADS
OFF_SG = OFF_B + 2 * DN_HEADS

LANES = 128
F32_SUBLANES = 8
VMEM_LIMIT_BYTES = 56 * 1024 * 1024

DN_CHUNK = 128
NEUMANN_BLOCK = 8
DN_BATCH_ROWS = 4
TOKEN_TILE = 512
INPROJ_TILE = 1024
INPROJ_SUBTILES = 4


def _rms(x, gain):
    return x * lax.rsqrt(jnp.mean(x * x, axis=-1, keepdims=True) + EPS) * gain


def _silu(x):
    h = 0.5 * x
    return h + h * jnp.tanh(h)


def _gelu_tanh(x):
    c = 0.7978845608028654
    h = 0.5 * x
    return h + h * jnp.tanh(x * (c + (c * 0.044715) * (x * x)))


def _bdot(a, b):
    return jnp.dot(a.astype(BF16), b.astype(BF16), preferred_element_type=F32)


def _conv_silu_norm(pre, prev_row, next_row, cw_ref, qkv_ref, kt_ref, row0):
    n, width = pre.shape
    sub = F32_SUBLANES
    first = lax.broadcasted_iota(jnp.int32, (sub, width), 0) == 0
    last = lax.broadcasted_iota(jnp.int32, (sub, width), 0) == sub - 1
    x_prev = pltpu.roll(pre, 1, 0)
    x_prev = jnp.concatenate([jnp.where(first, prev_row, x_prev[:sub]), x_prev[sub:]], axis=0)
    x_next = pltpu.roll(pre, n - 1, 0)
    x_next = jnp.concatenate([x_next[:n - sub], jnp.where(last, next_row, x_next[n - sub:])], axis=0)
    qkv = _silu(cw_ref[0:1, :] * x_prev + cw_ref[1:2, :] * pre + cw_ref[2:3, :] * x_next)
    rows = slice(row0, row0 + n)
    for h in range(2 * DN_HEADS):
        cols = slice(h * DN_HEAD_DIM, (h + 1) * DN_HEAD_DIM)
        t = qkv[:, cols]
        scale = DN_HEAD_DIM ** -0.5 if h < DN_HEADS else 1.0
        t = t * (lax.rsqrt(jnp.sum(t * t, axis=-1, keepdims=True) + EPS) * scale)
        qkv_ref[rows, cols] = t.astype(BF16)
        if h >= DN_HEADS:
            kcols = slice((h - DN_HEADS) * DN_HEAD_DIM, (h - DN_HEADS + 1) * DN_HEAD_DIM)
            for c in range(n // DN_CHUNK):
                kt_ref[(row0 + c * DN_CHUNK) // DN_CHUNK, kcols, :] = (
                    t[c * DN_CHUNK:(c + 1) * DN_CHUNK, :].T.astype(BF16))
    qkv_ref[rows, 2 * DN_WIDTH:] = qkv[:, 2 * DN_WIDTH:].astype(BF16)


def _spatial_gating(ps, lng_ref, lnb_ref, ws_ref, bst_ref, og_ref, yb_ref, row0):
    ps = _gelu_tanh(ps)
    u = ps[:, :SG_WIDTH]
    v = ps[:, SG_WIDTH:]
    vc = v - jnp.mean(v, axis=-1, keepdims=True)
    v = vc * lax.rsqrt(jnp.mean(vc * vc, axis=-1, keepdims=True) + EPS) * lng_ref[...] + lnb_ref[...]
    vb = v.astype(BF16)
    for n in range(ps.shape[0] // SG_CHUNK):
        rows = slice(n * SG_CHUNK, (n + 1) * SG_CHUNK)
        for g in range(SG_GROUPS):
            cols = slice(g * SG_GROUP_DIM, (g + 1) * SG_GROUP_DIM)
            mixed = jnp.dot(ws_ref[g], vb[rows, cols], preferred_element_type=F32) + bst_ref[:, g:g + 1]
            y = _rms(u[rows, cols] * mixed, og_ref[:, cols])
            yb_ref[row0 + n * SG_CHUNK:row0 + (n + 1) * SG_CHUNK, cols] = y.astype(BF16)


def _inproj_kernel(x_ref, xprev_ref, xnext_ref, ng_ref, wqkvz_ref, wab_ref, wsg_ref, lng_ref, lnb_ref,
                   ws_ref, bst_ref, og_ref, cw_ref, qkv_ref, kt_ref, z_ref, gates_ref, yb_ref, *,
                   tiles_per_seq):
    tm = x_ref.shape[0]
    st = tm // INPROJ_SUBTILES
    halo = xprev_ref.shape[0]
    sub_rows = [slice(s * st, (s + 1) * st) for s in range(INPROJ_SUBTILES)]
    h_halo = _rms(jnp.concatenate([xprev_ref[...], xnext_ref[...]], axis=0), ng_ref[...]).astype(BF16)
    hs = [_rms(x_ref[r, :], ng_ref[...]).astype(BF16) for r in sub_rows]
    pq0 = jnp.dot(jnp.concatenate([hs[0], h_halo], axis=0), wqkvz_ref[...], preferred_element_type=F32)
    pqs = [pq0[:st]] + [jnp.dot(h, wqkvz_ref[...], preferred_element_type=F32) for h in hs[1:]]
    pss = [jnp.dot(h, wsg_ref[...], preferred_element_type=F32) for h in hs]
    gates_ref[...] = jnp.dot(jnp.concatenate(hs, axis=0), wab_ref[...],
                             preferred_element_type=F32)

    pos = pl.program_id(0) % tiles_per_seq
    tile_prev = pq0[st + halo - 1:st + halo, :OFF_Z] * (pos > 0).astype(F32)
    tile_next = pq0[st + halo:st + halo + 1, :OFF_Z] * (pos < tiles_per_seq - 1).astype(F32)
    for s, r in enumerate(sub_rows):
        z_ref[r, :] = pqs[s][:, OFF_Z:].astype(BF16)
        prev_row = tile_prev if s == 0 else pqs[s - 1][st - 1:st, :OFF_Z]
        next_row = tile_next if s == INPROJ_SUBTILES - 1 else pqs[s + 1][0:1, :OFF_Z]
        _conv_silu_norm(pqs[s][:, :OFF_Z], prev_row, next_row, cw_ref, qkv_ref, kt_ref, s * st)
    for s in range(INPROJ_SUBTILES):
        _spatial_gating(pss[s], lng_ref, lnb_ref, ws_ref, bst_ref, og_ref, yb_ref, s * st)


def _inproj(x2, ng, w_in, wsg, lng, lnb, ws, bst, og, conv_w, seq, layer):
    t, d = x2.shape
    tm = INPROJ_TILE
    halo = F32_SUBLANES
    per_tile = tm // halo
    n_halo = t // halo
    once = pl.Buffered(1)
    full = lambda *shape: pl.BlockSpec(shape, lambda i: (0,) * len(shape), pipeline_mode=once)
    layer_cols = lambda width, col_block: pl.BlockSpec((None, d, width), lambda i: (layer, 0, col_block),
                                                       pipeline_mode=once)
    rows = lambda w: pl.BlockSpec((tm, w), lambda i: (i, 0))
    prev_spec = pl.BlockSpec((halo, d), lambda i: (jnp.maximum(i * per_tile - 1, 0), 0))
    next_spec = pl.BlockSpec((halo, d), lambda i: (jnp.minimum((i + 1) * per_tile, n_halo - 1), 0))
    return pl.pallas_call(
        functools.partial(_inproj_kernel, tiles_per_seq=seq // tm),
        grid=(t // tm,),
        in_specs=[rows(d), prev_spec, next_spec, full(1, d), layer_cols(OFF_A, 0),
                  layer_cols(LANES, OFF_A // LANES), layer_cols(2 * SG_WIDTH, 0),
                  full(1, SG_WIDTH), full(1, SG_WIDTH),
                  pl.BlockSpec((None, SG_GROUPS, SG_CHUNK, SG_CHUNK), lambda i: (layer, 0, 0, 0),
                               pipeline_mode=once),
                  full(SG_CHUNK, SG_GROUPS), full(1, SG_WIDTH), full(3, OFF_Z)],
        out_specs=[rows(OFF_Z), pl.BlockSpec((tm // DN_CHUNK, DN_WIDTH, DN_CHUNK), lambda i: (i, 0, 0)),
                   rows(DN_WIDTH), rows(LANES), rows(SG_WIDTH)],
        out_shape=[jax.ShapeDtypeStruct((t, OFF_Z), BF16),
                   jax.ShapeDtypeStruct((t // DN_CHUNK, DN_WIDTH, DN_CHUNK), BF16),
                   jax.ShapeDtypeStruct((t, DN_WIDTH), BF16),
                   jax.ShapeDtypeStruct((t, LANES), F32), jax.ShapeDtypeStruct((t, SG_WIDTH), BF16)],
        compiler_params=pltpu.CompilerParams(dimension_semantics=("arbitrary",),
                                             vmem_limit_bytes=VMEM_LIMIT_BYTES),
        name="inproj_sg",
    )(x2, x2, x2, ng, w_in, w_in, wsg, lng, lnb, ws, bst, og, conv_w)


def _blockdiag(p):
    n = p.shape[0]
    zero = jnp.zeros((n, n), p.dtype)
    return jnp.concatenate([jnp.concatenate([p[:, :n], zero], axis=1),
                            jnp.concatenate([zero, p[:, n:]], axis=1)], axis=0)


def _pdot(p, q):
    return jnp.dot(p.astype(BF16), _blockdiag(q.astype(BF16)), preferred_element_type=F32)


def _unit_tri_inverse(a2s):
    n = a2s[0].shape[0]
    row = lax.broadcasted_iota(jnp.int32, (n, 2 * n), 0)
    colm = lax.broadcasted_iota(jnp.int32, (n, 2 * n), 1) % n
    eye = (row == colm).astype(F32)
    base = NEUMANN_BLOCK
    base_mask = (row // base) == (colm // base)
    ld = [jnp.where(base_mask, a2, 0.0) for a2 in a2s]
    ld2 = [_pdot(x, x) for x in ld]
    t = [eye - x for x in ld]
    both = [_pdot(jnp.concatenate([x, ti], axis=0), x) for x, ti in zip(ld2, t)]
    ld4 = [bo[:n] for bo in both]
    t = [ti + bo[n:] for ti, bo in zip(t, both)]
    t = [ti + _pdot(ti, x) for ti, x in zip(t, ld4)]

    levels = []
    m = base
    while m < n:
        levels.append(((row // (2 * m)) == (colm // (2 * m))) & ((row // m) != (colm // m)))
        m *= 2
    if not levels:
        return t
    a_t = [_pdot(jnp.concatenate([jnp.where(joins, a2, 0.0) for joins in levels], axis=0), ti)
           for a2, ti in zip(a2s, t)]
    w = [[at[j * n:(j + 1) * n] for j in range(len(levels))] for at in a_t]
    for lvl in range(len(levels)):
        x = [wi[lvl] for wi in w]
        prod = [_pdot(jnp.concatenate([ti] + wi[lvl + 1:], axis=0), xi) for ti, wi, xi in zip(t, w, x)]
        t = [ti - pr[:n] for ti, pr in zip(t, prod)]
        w = [wi[:lvl + 1] + [wj - pr[(j + 1) * n:(j + 2) * n] for j, wj in enumerate(wi[lvl + 1:])]
             for wi, pr in zip(w, prod)]
    return t


def _chunk_cumsum(x, rows, reverse):
    n = x.shape[0]
    s = 1
    while s < n:
        if reverse:
            x = x + jnp.where(rows < n - s, pltpu.roll(x, n - s, 0), 0.0)
        else:
            x = x + jnp.where(rows >= s, pltpu.roll(x, s, 0), 0.0)
        s *= 2
    return x


def _delta_operands(reverse, qkv, kt_all, gates, alog_ref, dtb_ref):
    c = DN_CHUNK
    row = lax.broadcasted_iota(jnp.int32, (c, c), 0)
    col = lax.broadcasted_iota(jnp.int32, (c, c), 1)
    mxu_dtype = qkv.dtype

    z = gates + dtb_ref[...]
    softplus = jnp.maximum(z, 0.0) + jnp.log(1.0 + jnp.exp(-jnp.abs(z)))
    g = -jnp.exp(alog_ref[...]) * softplus
    beta = jax.nn.sigmoid(gates)
    gc = _chunk_cumsum(g, row, reverse)
    gct = gc.T
    g_last = gc[0:1, :] if reverse else gc[c - 1:c, :]
    g_last_t = gct[:, 0:1] if reverse else gct[:, c - 1:c]
    e_gc = jnp.exp(gc)
    e_tail_t = jnp.exp(g_last_t - gct).astype(mxu_dtype)
    e_last = jnp.exp(g_last)

    incl = (row <= col) if reverse else (row >= col)
    strict = (row < col) if reverse else (row > col)
    d = 1 if reverse else 0

    chains = []
    for h in range(DN_HEADS):
        r = d * DN_HEADS + h
        q = qkv[:, h * DN_HEAD_DIM:(h + 1) * DN_HEAD_DIM]
        k = qkv[:, DN_WIDTH + h * DN_HEAD_DIM:DN_WIDTH + (h + 1) * DN_HEAD_DIM]
        v = qkv[:, 2 * DN_WIDTH + h * DN_HEAD_DIM:2 * DN_WIDTH + (h + 1) * DN_HEAD_DIM]
        kt = kt_all[h * DN_HEAD_DIM:(h + 1) * DN_HEAD_DIM, :]
        b_bc = jnp.broadcast_to(beta[:, 2 * DN_HEADS + r:2 * DN_HEADS + r + 1],
                                (c, DN_HEAD_DIM)).astype(mxu_dtype)
        eg_bc = jnp.broadcast_to(e_gc[:, r:r + 1], (c, DN_HEAD_DIM)).astype(mxu_dtype)
        kb = k * b_bc
        chains.append(dict(
            reverse=reverse, strict=strict, kt=kt,
            gram_lhs=jnp.concatenate([kb, q], axis=0),
            decay=jnp.exp(jnp.where(incl, gc[:, r:r + 1] - gct[r:r + 1, :], -1e30)),
            rhs=jnp.concatenate([v * b_bc, kb * eg_bc], axis=1),
            qdec=q * eg_bc, ktail=kt * e_tail_t[r:r + 1, :],
            e_last=jnp.broadcast_to(e_last[:, r:r + 1], (1, DN_HEAD_DIM))))
    return chains


def _wy_solve(a_list, rhs_list, rev_list):
    c = DN_CHUNK
    hc = c // 2
    lane = lax.broadcasted_iota(jnp.int32, (hc, c), 1)
    diag = [jnp.where(lane < hc, a[:hc], a[hc:]) for a in a_list]
    t12 = _unit_tri_inverse(diag)
    t21 = [pltpu.roll(t, hc, 1) for t in t12]
    t_first = [(t21 if rev else t12)[i][:, :hc] for i, rev in enumerate(rev_list)]
    t_second = [(t12 if rev else t21)[i][:, :hc] for i, rev in enumerate(rev_list)]
    a_off = [pltpu.roll(a[:hc], hc, 1)[:, :hc] if rev else a[hc:, :hc] for a, rev in zip(a_list, rev_list)]
    r_first = [r_[hc:] if rev else r_[:hc] for r_, rev in zip(rhs_list, rev_list)]
    r_second = [r_[:hc] if rev else r_[hc:] for r_, rev in zip(rhs_list, rev_list)]
    x_first = [_bdot(t, r_) for t, r_ in zip(t_first, r_first)]
    y = [_bdot(a, x) for a, x in zip(a_off, x_first)]
    x_second = [_bdot(t, r_ - y_) for t, r_, y_ in zip(t_second, r_second, y)]
    return [jnp.concatenate([x2, x1] if rev else [x1, x2], axis=0)
            for x1, x2, rev in zip(x_first, x_second, rev_list)]


def _delta_chunks(chains, states):
    c = DN_CHUNK
    grams = [_bdot(p["gram_lhs"], p["kt"]) for p in chains]
    a_s = [jnp.where(p["strict"], gm[:c] * p["decay"], 0.0) for p, gm in zip(chains, grams)]
    attns = [gm[c:] * p["decay"] for p, gm in zip(chains, grams)]
    uw = _wy_solve(a_s, [p["rhs"] for p in chains], [p["reverse"] for p in chains])
    rss = [_bdot(jnp.concatenate([x[:, DN_HEAD_DIM:].astype(BF16), p["qdec"].astype(BF16)], axis=0), s_)
           for p, x, s_ in zip(chains, uw, states)]
    v_news = [x[:, :DN_HEAD_DIM] - rs[:c] for x, rs in zip(uw, rss)]
    rvs = [_bdot(jnp.concatenate([at.astype(BF16), p["ktail"].astype(BF16)], axis=0), vn)
           for p, at, vn in zip(chains, attns, v_news)]
    outs = [rs[c:] + rv[:c] for rs, rv in zip(rss, rvs)]
    new_states = [s_ * p["e_last"] + rv[c:] for p, s_, rv in zip(chains, states, rvs)]
    return outs, new_states


def _delta_kernel(xf_ref, xb_ref, ktf_ref, ktb_ref, gf_ref, gb_ref, alog_ref, dtb_ref, of_ref, ob_ref, s_ref):
    n_rows = xf_ref.shape[0]

    @pl.when(pl.program_id(1) == 0)
    def _():
        s_ref[...] = jnp.zeros_like(s_ref)

    chains = []
    for b in range(n_rows):
        chains += _delta_operands(False, xf_ref[b], ktf_ref[b, 0], gf_ref[b], alog_ref, dtb_ref)
        chains += _delta_operands(True, xb_ref[b], ktb_ref[b, 0], gb_ref[b], alog_ref, dtb_ref)
    outs, new_states = _delta_chunks(chains, [s_ref[i] for i in range(len(chains))])
    for i, (o, s_) in enumerate(zip(outs, new_states)):
        b, h = i // (2 * DN_HEADS), i % DN_HEADS
        o_ref = ob_ref if chains[i]["reverse"] else of_ref
        o_ref[b, :, h * DN_HEAD_DIM:(h + 1) * DN_HEAD_DIM] = o
        s_ref[i] = s_


def _delta_rule(qkv, kt, gates, alog_row, dtb_row):
    bsz, seq, width = qkv.shape
    c = DN_CHUNK
    nb = DN_BATCH_ROWS
    nc = seq // c
    fwd = lambda n: n
    bwd = lambda n: nc - 1 - n
    chunk_spec = lambda w, chunk_of: pl.BlockSpec((nb, c, w), lambda b, n: (b, chunk_of(n), 0))
    kt_spec = lambda chunk_of: pl.BlockSpec((nb, 1, DN_WIDTH, c), lambda b, n: (b, chunk_of(n), 0, 0))
    full = lambda *shape: pl.BlockSpec(shape, lambda b, n: (0,) * len(shape))
    out_sds = jax.ShapeDtypeStruct((bsz, seq, DN_WIDTH), F32)
    return pl.pallas_call(
        _delta_kernel,
        grid=(bsz // nb, nc),
        in_specs=[chunk_spec(width, fwd), chunk_spec(width, bwd), kt_spec(fwd), kt_spec(bwd),
                  chunk_spec(LANES, fwd), chunk_spec(LANES, bwd), full(1, LANES), full(1, LANES)],
        out_specs=[chunk_spec(DN_WIDTH, fwd), chunk_spec(DN_WIDTH, bwd)],
        out_shape=[out_sds, out_sds],
        scratch_shapes=[pltpu.VMEM((2 * nb * DN_HEADS, DN_HEAD_DIM, DN_HEAD_DIM), F32)],
        compiler_params=pltpu.CompilerParams(dimension_semantics=("arbitrary", "arbitrary"),
                                             vmem_limit_bytes=VMEM_LIMIT_BYTES),
        name="delta_rule",
    )(qkv, qkv, kt, kt, gates, gates, alog_row, dtb_row)


def _mix_ffn_kernel(x_ref, of_ref, ob_ref, z_ref, yb_ref, dng_ref, wo_ref, fg_ref, wg_ref, wu_ref, wd_ref,
                    fin_ref, out_ref, *, final):
    x1 = x_ref[...] + jnp.dot(yb_ref[...], wo_ref[DN_WIDTH:, :], preferred_element_type=F32)
    o = of_ref[...] + ob_ref[...]
    z = z_ref[...].astype(F32)
    heads = []
    for h in range(DN_HEADS):
        cols = slice(h * DN_HEAD_DIM, (h + 1) * DN_HEAD_DIM)
        heads.append((_rms(o[:, cols], dng_ref[...]) * _silu(z[:, cols])).astype(BF16))
    x1 = x1 + jnp.dot(jnp.concatenate(heads, axis=1), wo_ref[:DN_WIDTH, :], preferred_element_type=F32)
    hb = _rms(x1, fg_ref[...]).astype(BF16)
    gate = jnp.dot(hb, wg_ref[...], preferred_element_type=F32)
    up = jnp.dot(hb, wu_ref[...], preferred_element_type=F32)
    hid = (_silu(gate) * up).astype(BF16)
    out = x1 + jnp.dot(hid, wd_ref[...], preferred_element_type=F32)
    if final:
        out = _rms(out, fin_ref[...])
    out_ref[...] = out


def _mix_ffn(x2, o_f, o_b, z, yb, dng, wo, fg, wg, wu, wd, fin, final, layer):
    t, d = x2.shape
    f = wg.shape[2]
    tm = TOKEN_TILE
    once = pl.Buffered(1)
    full = lambda *shape: pl.BlockSpec(shape, lambda i: (0,) * len(shape), pipeline_mode=once)
    of_layer = lambda r, c: pl.BlockSpec((None, r, c), lambda i: (layer, 0, 0), pipeline_mode=once)
    rows = lambda w: pl.BlockSpec((tm, w), lambda i: (i, 0))
    return pl.pallas_call(
        functools.partial(_mix_ffn_kernel, final=final),
        grid=(t // tm,),
        in_specs=[rows(d), rows(DN_WIDTH), rows(DN_WIDTH), rows(DN_WIDTH), rows(SG_WIDTH),
                  full(1, DN_HEAD_DIM), of_layer(DN_WIDTH + SG_WIDTH, d), full(1, d), of_layer(d, f),
                  of_layer(d, f), of_layer(f, d), full(1, d)],
        out_specs=rows(d),
        out_shape=jax.ShapeDtypeStruct((t, d), F32),
        compiler_params=pltpu.CompilerParams(dimension_semantics=("arbitrary",),
                                             vmem_limit_bytes=VMEM_LIMIT_BYTES),
        name="mix_ffn",
    )(x2, o_f, o_b, z, yb, dng, wo, fg, wg, wu, wd, fin)


def _pad_lanes(v):
    flat = v.reshape(1, -1).astype(F32)
    return jnp.pad(flat, ((0, 0), (0, LANES - flat.shape[1])))


def kernel(x, mix_norm_g, w_in, conv_w, dn_a_log, dn_dt_bias, dn_norm_g, sg_ln_g, sg_ln_b, sg_w, sg_b,
           sg_out_g, w_out, ffn_norm_g, w_gate, w_up, w_down, final_norm_g):
    bsz, seq, d = x.shape
    depth = w_in.shape[0]
    assert seq % TOKEN_TILE == 0 and seq % INPROJ_TILE == 0 and seq % DN_CHUNK == 0
    assert (INPROJ_TILE // INPROJ_SUBTILES) % SG_CHUNK == 0
    assert bsz % DN_BATCH_ROWS == 0
    x2 = x.reshape(bsz * seq, d)
    row = lambda v: v.reshape(1, -1).astype(F32)
    w_in_b = w_in.astype(BF16)
    wsg_b = w_in_b[:, :, OFF_SG:]
    sg_w_b = sg_w.astype(BF16)
    w_out_b, w_gate_b, w_up_b, w_down_b = (w.astype(BF16) for w in (w_out, w_gate, w_up, w_down))
    for l in range(depth):
        qkv, kt, z, gates, yb = _inproj(x2, row(mix_norm_g[l]), w_in_b, wsg_b, row(sg_ln_g[l]),
                                        row(sg_ln_b[l]), sg_w_b, sg_b[l].T.astype(F32),
                                        row(sg_out_g[l]), conv_w[l].astype(F32), seq, l)
        o_f, o_b = _delta_rule(qkv.reshape(bsz, seq, OFF_Z),
                               kt.reshape(bsz, seq // DN_CHUNK, DN_WIDTH, DN_CHUNK),
                               gates.reshape(bsz, seq, LANES),
                               _pad_lanes(dn_a_log[l]), _pad_lanes(dn_dt_bias[l]))
        x2 = _mix_ffn(x2, o_f.reshape(bsz * seq, DN_WIDTH), o_b.reshape(bsz * seq, DN_WIDTH), z, yb,
                      row(dn_norm_g[l]), w_out_b, row(ffn_norm_g[l]), w_gate_b, w_up_b, w_down_b,
                      row(final_norm_g), final=(l == depth - 1), layer=l)
    return x2.reshape(bsz, seq, d)
```

```python
import functools

import jax
import jax.numpy as jnp
from jax import lax
from jax.experimental import pallas as pl
from jax.experimental.pallas import tpu as pltpu

F32 = jnp.float32
BF16 = jnp.bfloat16

DN_HEADS = 4
DN_HEAD_DIM = 128
DN_WIDTH = DN_HEADS * DN_HEAD_DIM
SG_GROUPS = 4
SG_GROUP_DIM = 128
SG_WIDTH = SG_GROUPS * SG_GROUP_DIM
SG_CHUNK = 128
EPS = 1e-6

OFF_Z = 3 * DN_WIDTH
OFF_A = 4 * DN_WIDTH
OFF_B = OFF_A + 2 * DN_HEADS
OFF_SG = OFF_B + 2 * DN_HEADS

LANES = 128
F32_SUBLANES = 8
VMEM_LIMIT_BYTES = 56 * 1024 * 1024

DN_CHUNK = 128
NEUMANN_BLOCK = 8
DN_BATCH_ROWS = 4
DN_STEP_CHUNKS = 2
TOKEN_TILE = 512
INPROJ_TILE = 1024
INPROJ_SUBTILES = 4


def _rms(x, gain):
    return x * lax.rsqrt(jnp.mean(x * x, axis=-1, keepdims=True) + EPS) * gain


def _silu(x):
    h = 0.5 * x
    return h + h * jnp.tanh(h)


def _gelu_tanh(x):
    c = 0.7978845608028654
    h = 0.5 * x
    return h + h * jnp.tanh(x * (c + (c * 0.044715) * (x * x)))


def _bdot(a, b):
    return jnp.dot(a.astype(BF16), b.astype(BF16), preferred_element_type=F32)


def _conv_silu_norm(pre, prev_row, next_row, cw_ref, qkv_ref, kt_ref, row0):
    n, width = pre.shape
    sub = F32_SUBLANES
    first = lax.broadcasted_iota(jnp.int32, (sub, width), 0) == 0
    last = lax.broadcasted_iota(jnp.int32, (sub, width), 0) == sub - 1
    x_prev = pltpu.roll(pre, 1, 0)
    x_prev = jnp.concatenate([jnp.where(first, prev_row, x_prev[:sub]), x_prev[sub:]], axis=0)
    x_next = pltpu.roll(pre, n - 1, 0)
    x_next = jnp.concatenate([x_next[:n - sub], jnp.where(last, next_row, x_next[n - sub:])], axis=0)
    qkv = _silu(cw_ref[0:1, :] * x_prev + cw_ref[1:2, :] * pre + cw_ref[2:3, :] * x_next)
    rows = slice(row0, row0 + n)
    for h in range(2 * DN_HEADS):
        cols = slice(h * DN_HEAD_DIM, (h + 1) * DN_HEAD_DIM)
        t = qkv[:, cols]
        scale = DN_HEAD_DIM ** -0.5 if h < DN_HEADS else 1.0
        t = t * (lax.rsqrt(jnp.sum(t * t, axis=-1, keepdims=True) + EPS) * scale)
        qkv_ref[rows, cols] = t.astype(BF16)
        if h >= DN_HEADS:
            kcols = slice((h - DN_HEADS) * DN_HEAD_DIM, (h - DN_HEADS + 1) * DN_HEAD_DIM)
            for c in range(n // DN_CHUNK):
                kt_ref[(row0 + c * DN_CHUNK) // DN_CHUNK, kcols, :] = (
                    t[c * DN_CHUNK:(c + 1) * DN_CHUNK, :].T.astype(BF16))
    qkv_ref[rows, 2 * DN_WIDTH:] = qkv[:, 2 * DN_WIDTH:].astype(BF16)


def _spatial_gating(ps, lng_ref, lnb_ref, ws_ref, bst_ref, og_ref, yb_ref, row0):
    ps = _gelu_tanh(ps)
    u = ps[:, :SG_WIDTH]
    v = ps[:, SG_WIDTH:]
    vc = v - jnp.mean(v, axis=-1, keepdims=True)
    v = vc * lax.rsqrt(jnp.mean(vc * vc, axis=-1, keepdims=True) + EPS) * lng_ref[...] + lnb_ref[...]
    vb = v.astype(BF16)
    for n in range(ps.shape[0] // SG_CHUNK):
        rows = slice(n * SG_CHUNK, (n + 1) * SG_CHUNK)
        for g in range(SG_GROUPS):
            cols = slice(g * SG_GROUP_DIM, (g + 1) * SG_GROUP_DIM)
            mixed = jnp.dot(ws_ref[g], vb[rows, cols], preferred_element_type=F32) + bst_ref[:, g:g + 1]
            y = _rms(u[rows, cols] * mixed, og_ref[:, cols])
            yb_ref[row0 + n * SG_CHUNK:row0 + (n + 1) * SG_CHUNK, cols] = y.astype(BF16)


def _inproj_kernel(x_ref, xprev_ref, xnext_ref, ng_ref, wqkvz_ref, wab_ref, wsg_ref, lng_ref, lnb_ref,
                   ws_ref, bst_ref, og_ref, cw_ref, qkv_ref, kt_ref, z_ref, gates_ref, yb_ref, *,
                   tiles_per_seq):
    tm = x_ref.shape[0]
    st = tm // INPROJ_SUBTILES
    halo = xprev_ref.shape[0]
    sub_rows = [slice(s * st, (s + 1) * st) for s in range(INPROJ_SUBTILES)]
    h_halo = _rms(jnp.concatenate([xprev_ref[...], xnext_ref[...]], axis=0), ng_ref[...]).astype(BF16)
    hs = [_rms(x_ref[r, :], ng_ref[...]).astype(BF16) for r in sub_rows]
    pq0 = jnp.dot(jnp.concatenate([hs[0], h_halo], axis=0), wqkvz_ref[...], preferred_element_type=F32)
    pqs = [pq0[:st]] + [jnp.dot(h, wqkvz_ref[...], preferred_element_type=F32) for h in hs[1:]]
    pss = [jnp.dot(h, wsg_ref[...], preferred_element_type=F32) for h in hs]
    gates_ref[...] = jnp.dot(jnp.concatenate(hs, axis=0), wab_ref[...],
                             preferred_element_type=F32)

    pos = pl.program_id(0) % tiles_per_seq
    tile_prev = pq0[st + halo - 1:st + halo, :OFF_Z] * (pos > 0).astype(F32)
    tile_next = pq0[st + halo:st + halo + 1, :OFF_Z] * (pos < tiles_per_seq - 1).astype(F32)
    for s, r in enumerate(sub_rows):
        z_ref[r, :] = pqs[s][:, OFF_Z:].astype(BF16)
        prev_row = tile_prev if s == 0 else pqs[s - 1][st - 1:st, :OFF_Z]
        next_row = tile_next if s == INPROJ_SUBTILES - 1 else pqs[s + 1][0:1, :OFF_Z]
        _conv_silu_norm(pqs[s][:, :OFF_Z], prev_row, next_row, cw_ref, qkv_ref, kt_ref, s * st)
    for s in range(INPROJ_SUBTILES):
        _spatial_gating(pss[s], lng_ref, lnb_ref, ws_ref, bst_ref, og_ref, yb_ref, s * st)


def _inproj(x2, ng, w_in, wsg, lng, lnb, ws, bst, og, conv_w, seq, layer):
    t, d = x2.shape
    tm = INPROJ_TILE
    halo = F32_SUBLANES
    per_tile = tm // halo
    n_halo = t // halo
    once = pl.Buffered(1)
    full = lambda *shape: pl.BlockSpec(shape, lambda i: (0,) * len(shape), pipeline_mode=once)
    layer_cols = lambda width, col_block: pl.BlockSpec((None, d, width), lambda i: (layer, 0, col_block),
                                                       pipeline_mode=once)
    rows = lambda w: pl.BlockSpec((tm, w), lambda i: (i, 0))
    prev_spec = pl.BlockSpec((halo, d), lambda i: (jnp.maximum(i * per_tile - 1, 0), 0))
    next_spec = pl.BlockSpec((halo, d), lambda i: (jnp.minimum((i + 1) * per_tile, n_halo - 1), 0))
    return pl.pallas_call(
        functools.partial(_inproj_kernel, tiles_per_seq=seq // tm),
        grid=(t // tm,),
        in_specs=[rows(d), prev_spec, next_spec, full(1, d), layer_cols(OFF_A, 0),
                  layer_cols(LANES, OFF_A // LANES), layer_cols(2 * SG_WIDTH, 0),
                  full(1, SG_WIDTH), full(1, SG_WIDTH),
                  pl.BlockSpec((None, SG_GROUPS, SG_CHUNK, SG_CHUNK), lambda i: (layer, 0, 0, 0),
                               pipeline_mode=once),
                  full(SG_CHUNK, SG_GROUPS), full(1, SG_WIDTH), full(3, OFF_Z)],
        out_specs=[rows(OFF_Z), pl.BlockSpec((tm // DN_CHUNK, DN_WIDTH, DN_CHUNK), lambda i: (i, 0, 0)),
                   rows(DN_WIDTH), rows(LANES), rows(SG_WIDTH)],
        out_shape=[jax.ShapeDtypeStruct((t, OFF_Z), BF16),
                   jax.ShapeDtypeStruct((t // DN_CHUNK, DN_WIDTH, DN_CHUNK), BF16),
                   jax.ShapeDtypeStruct((t, DN_WIDTH), BF16),
                   jax.ShapeDtypeStruct((t, LANES), F32), jax.ShapeDtypeStruct((t, SG_WIDTH), BF16)],
        compiler_params=pltpu.CompilerParams(dimension_semantics=("arbitrary",),
                                             vmem_limit_bytes=VMEM_LIMIT_BYTES),
        name="inproj_sg",
    )(x2, x2, x2, ng, w_in, w_in, wsg, lng, lnb, ws, bst, og, conv_w)


def _blockdiag(p):
    n = p.shape[0]
    zero = jnp.zeros((n, n), p.dtype)
    return jnp.concatenate([jnp.concatenate([p[:, :n], zero], axis=1),
                            jnp.concatenate([zero, p[:, n:]], axis=1)], axis=0)


def _pdot(p, q):
    return jnp.dot(p.astype(BF16), _blockdiag(q.astype(BF16)), preferred_element_type=F32)


def _unit_tri_inverse(a2s):
    n = a2s[0].shape[0]
    row = lax.broadcasted_iota(jnp.int32, (n, 2 * n), 0)
    colm = lax.broadcasted_iota(jnp.int32, (n, 2 * n), 1) % n
    eye = (row == colm).astype(F32)
    base = NEUMANN_BLOCK
    base_mask = (row // base) == (colm // base)
    ld = [jnp.where(base_mask, a2, 0.0) for a2 in a2s]
    ld2 = [_pdot(x, x) for x in ld]
    t = [eye - x for x in ld]
    both = [_pdot(jnp.concatenate([x, ti], axis=0), x) for x, ti in zip(ld2, t)]
    ld4 = [bo[:n] for bo in both]
    t = [ti + bo[n:] for ti, bo in zip(t, both)]
    t = [ti + _pdot(ti, x) for ti, x in zip(t, ld4)]

    levels = []
    m = base
    while m < n:
        levels.append(((row // (2 * m)) == (colm // (2 * m))) & ((row // m) != (colm // m)))
        m *= 2
    if not levels:
        return t
    a_t = [_pdot(jnp.concatenate([jnp.where(joins, a2, 0.0) for joins in levels], axis=0), ti)
           for a2, ti in zip(a2s, t)]
    w = [[at[j * n:(j + 1) * n] for j in range(len(levels))] for at in a_t]
    for lvl in range(len(levels)):
        x = [wi[lvl] for wi in w]
        prod = [_pdot(jnp.concatenate([ti] + wi[lvl + 1:], axis=0), xi) for ti, wi, xi in zip(t, w, x)]
        t = [ti - pr[:n] for ti, pr in zip(t, prod)]
        w = [wi[:lvl + 1] + [wj - pr[(j + 1) * n:(j + 2) * n] for j, wj in enumerate(wi[lvl + 1:])]
             for wi, pr in zip(w, prod)]
    return t


def _chunk_cumsum(x, rows, reverse):
    n = x.shape[0]
    s = 1
    while s < n:
        if reverse:
            x = x + jnp.where(rows < n - s, pltpu.roll(x, n - s, 0), 0.0)
        else:
            x = x + jnp.where(rows >= s, pltpu.roll(x, s, 0), 0.0)
        s *= 2
    return x


def _delta_operands(reverse, qkv, kt_all, gates, alog_ref, dtb_ref):
    c = DN_CHUNK
    row = lax.broadcasted_iota(jnp.int32, (c, c), 0)
    col = lax.broadcasted_iota(jnp.int32, (c, c), 1)
    mxu_dtype = qkv.dtype

    z = gates + dtb_ref[...]
    softplus = jnp.maximum(z, 0.0) + jnp.log(1.0 + jnp.exp(-jnp.abs(z)))
    g = -jnp.exp(alog_ref[...]) * softplus
    beta = jax.nn.sigmoid(gates)
    gc = _chunk_cumsum(g, row, reverse)
    gct = gc.T
    g_last = gc[0:1, :] if reverse else gc[c - 1:c, :]
    g_last_t = gct[:, 0:1] if reverse else gct[:, c - 1:c]
    e_gc = jnp.exp(gc)
    e_tail_t = jnp.exp(g_last_t - gct).astype(mxu_dtype)
    e_last = jnp.exp(g_last)

    incl = (row <= col) if reverse else (row >= col)
    strict = (row < col) if reverse else (row > col)
    d = 1 if reverse else 0

    chains = []
    for h in range(DN_HEADS):
        r = d * DN_HEADS + h
        q = qkv[:, h * DN_HEAD_DIM:(h + 1) * DN_HEAD_DIM]
        k = qkv[:, DN_WIDTH + h * DN_HEAD_DIM:DN_WIDTH + (h + 1) * DN_HEAD_DIM]
        v = qkv[:, 2 * DN_WIDTH + h * DN_HEAD_DIM:2 * DN_WIDTH + (h + 1) * DN_HEAD_DIM]
        kt = kt_all[h * DN_HEAD_DIM:(h + 1) * DN_HEAD_DIM, :]
        b_bc = jnp.broadcast_to(beta[:, 2 * DN_HEADS + r:2 * DN_HEADS + r + 1],
                                (c, DN_HEAD_DIM)).astype(mxu_dtype)
        eg_bc = jnp.broadcast_to(e_gc[:, r:r + 1], (c, DN_HEAD_DIM)).astype(mxu_dtype)
        kb = k * b_bc
        chains.append(dict(
            reverse=reverse, strict=strict, kt=kt,
            gram_lhs=jnp.concatenate([kb, q], axis=0),
            decay=jnp.exp(jnp.where(incl, gc[:, r:r + 1] - gct[r:r + 1, :], -1e30)),
            rhs=jnp.concatenate([v * b_bc, kb * eg_bc], axis=1),
            qdec=q * eg_bc, ktail=kt * e_tail_t[r:r + 1, :],
            e_last=jnp.broadcast_to(e_last[:, r:r + 1], (1, DN_HEAD_DIM))))
    return chains


def _wy_solve(a_list, rhs_list, rev_list):
    c = DN_CHUNK
    hc = c // 2
    lane = lax.broadcasted_iota(jnp.int32, (hc, c), 1)
    diag = [jnp.where(lane < hc, a[:hc], a[hc:]) for a in a_list]
    t12 = _unit_tri_inverse(diag)
    t21 = [pltpu.roll(t, hc, 1) for t in t12]
    t_first = [(t21 if rev else t12)[i][:, :hc] for i, rev in enumerate(rev_list)]
    t_second = [(t12 if rev else t21)[i][:, :hc] for i, rev in enumerate(rev_list)]
    a_off = [pltpu.roll(a[:hc], hc, 1)[:, :hc] if rev else a[hc:, :hc] for a, rev in zip(a_list, rev_list)]
    r_first = [r_[hc:] if rev else r_[:hc] for r_, rev in zip(rhs_list, rev_list)]
    r_second = [r_[:hc] if rev else r_[hc:] for r_, rev in zip(rhs_list, rev_list)]
    x_first = [_bdot(t, r_) for t, r_ in zip(t_first, r_first)]
    y = [_bdot(a, x) for a, x in zip(a_off, x_first)]
    x_second = [_bdot(t, r_ - y_) for t, r_, y_ in zip(t_second, r_second, y)]
    return [jnp.concatenate([x2, x1] if rev else [x1, x2], axis=0)
            for x1, x2, rev in zip(x_first, x_second, rev_list)]


def _delta_independent(chains):
    c = DN_CHUNK
    grams = [_bdot(p["gram_lhs"], p["kt"]) for p in chains]
    a_s = [jnp.where(p["strict"], gm[:c] * p["decay"], 0.0) for p, gm in zip(chains, grams)]
    attns = [gm[c:] * p["decay"] for p, gm in zip(chains, grams)]
    uw = _wy_solve(a_s, [p["rhs"] for p in chains], [p["reverse"] for p in chains])
    return [dict(u=x[:, :DN_HEAD_DIM],
                 wq=jnp.concatenate([x[:, DN_HEAD_DIM:].astype(BF16), p["qdec"].astype(BF16)], axis=0),
                 ak=jnp.concatenate([at.astype(BF16), p["ktail"].astype(BF16)], axis=0),
                 e_last=p["e_last"])
            for p, x, at in zip(chains, uw, attns)]


def _delta_recurrent(parts, states):
    c = DN_CHUNK
    rss = [_bdot(p["wq"], s_) for p, s_ in zip(parts, states)]
    v_news = [p["u"] - rs[:c] for p, rs in zip(parts, rss)]
    rvs = [_bdot(p["ak"], vn) for p, vn in zip(parts, v_news)]
    outs = [rs[c:] + rv[:c] for rs, rv in zip(rss, rvs)]
    new_states = [s_ * p["e_last"] + rv[c:] for p, s_, rv in zip(parts, states, rvs)]
    return outs, new_states


def _delta_kernel(xf_ref, xb_ref, ktf_ref, ktb_ref, gf_ref, gb_ref, alog_ref, dtb_ref, of_ref, ob_ref, s_ref):
    n_rows = xf_ref.shape[0]
    c = DN_CHUNK
    step_chunks = xf_ref.shape[1] // c

    @pl.when(pl.program_id(1) == 0)
    def _():
        s_ref[...] = jnp.zeros_like(s_ref)

    chains, places = [], []
    for j in range(step_chunks):
        jb = step_chunks - 1 - j
        for b in range(n_rows):
            chains += _delta_operands(False, xf_ref[b, j * c:(j + 1) * c, :], ktf_ref[b, j],
                                      gf_ref[b, j * c:(j + 1) * c, :], alog_ref, dtb_ref)
            places += [(of_ref, b, j, h) for h in range(DN_HEADS)]
            chains += _delta_operands(True, xb_ref[b, jb * c:(jb + 1) * c, :], ktb_ref[b, jb],
                                      gb_ref[b, jb * c:(jb + 1) * c, :], alog_ref, dtb_ref)
            places += [(ob_ref, b, jb, h) for h in range(DN_HEADS)]
    parts = _delta_independent(chains)
    per_chunk = len(chains) // step_chunks
    states = [s_ref[i] for i in range(per_chunk)]
    for j in range(step_chunks):
        sel = slice(j * per_chunk, (j + 1) * per_chunk)
        outs, states = _delta_recurrent(parts[sel], states)
        for o, (o_ref, b, jj, h) in zip(outs, places[sel]):
            o_ref[b, jj * c:(jj + 1) * c, h * DN_HEAD_DIM:(h + 1) * DN_HEAD_DIM] = o
    for i, s_ in enumerate(states):
        s_ref[i] = s_


def _delta_rule(qkv, kt, gates, alog_row, dtb_row):
    bsz, seq, width = qkv.shape
    c = DN_CHUNK
    nb = DN_BATCH_ROWS
    sc = DN_STEP_CHUNKS
    n_steps = seq // (sc * c)
    fwd = lambda n: n
    bwd = lambda n: n_steps - 1 - n
    chunk_spec = lambda w, block_of: pl.BlockSpec((nb, sc * c, w), lambda b, n: (b, block_of(n), 0))
    kt_spec = lambda block_of: pl.BlockSpec((nb, sc, DN_WIDTH, c), lambda b, n: (b, block_of(n), 0, 0))
    full = lambda *shape: pl.BlockSpec(shape, lambda b, n: (0,) * len(shape))
    out_sds = jax.ShapeDtypeStruct((bsz, seq, DN_WIDTH), F32)
    return pl.pallas_call(
        _delta_kernel,
        grid=(bsz // nb, n_steps),
        in_specs=[chunk_spec(width, fwd), chunk_spec(width, bwd), kt_spec(fwd), kt_spec(bwd),
                  chunk_spec(LANES, fwd), chunk_spec(LANES, bwd), full(1, LANES), full(1, LANES)],
        out_specs=[chunk_spec(DN_WIDTH, fwd), chunk_spec(DN_WIDTH, bwd)],
        out_shape=[out_sds, out_sds],
        scratch_shapes=[pltpu.VMEM((2 * nb * DN_HEADS, DN_HEAD_DIM, DN_HEAD_DIM), F32)],
        compiler_params=pltpu.CompilerParams(dimension_semantics=("arbitrary", "arbitrary"),
                                             vmem_limit_bytes=VMEM_LIMIT_BYTES),
        name="delta_rule",
    )(qkv, qkv, kt, kt, gates, gates, alog_row, dtb_row)


def _mix_ffn_kernel(x_ref, of_ref, ob_ref, z_ref, yb_ref, dng_ref, wo_ref, fg_ref, wg_ref, wu_ref, wd_ref,
                    fin_ref, out_ref, *, final):
    x1 = x_ref[...] + jnp.dot(yb_ref[...], wo_ref[DN_WIDTH:, :], preferred_element_type=F32)
    o = of_ref[...] + ob_ref[...]
    z = z_ref[...].astype(F32)
    heads = []
    for h in range(DN_HEADS):
        cols = slice(h * DN_HEAD_DIM, (h + 1) * DN_HEAD_DIM)
        heads.append((_rms(o[:, cols], dng_ref[...]) * _silu(z[:, cols])).astype(BF16))
    x1 = x1 + jnp.dot(jnp.concatenate(heads, axis=1), wo_ref[:DN_WIDTH, :], preferred_element_type=F32)
    hb = _rms(x1, fg_ref[...]).astype(BF16)
    gate = jnp.dot(hb, wg_ref[...], preferred_element_type=F32)
    up = jnp.dot(hb, wu_ref[...], preferred_element_type=F32)
    hid = (_silu(gate) * up).astype(BF16)
    out = x1 + jnp.dot(hid, wd_ref[...], preferred_element_type=F32)
    if final:
        out = _rms(out, fin_ref[...])
    out_ref[...] = out


def _mix_ffn(x2, o_f, o_b, z, yb, dng, wo, fg, wg, wu, wd, fin, final, layer):
    t, d = x2.shape
    f = wg.shape[2]
    tm = TOKEN_TILE
    once = pl.Buffered(1)
    full = lambda *shape: pl.BlockSpec(shape, lambda i: (0,) * len(shape), pipeline_mode=once)
    of_layer = lambda r, c: pl.BlockSpec((None, r, c), lambda i: (layer, 0, 0), pipeline_mode=once)
    rows = lambda w: pl.BlockSpec((tm, w), lambda i: (i, 0))
    return pl.pallas_call(
        functools.partial(_mix_ffn_kernel, final=final),
        grid=(t // tm,),
        in_specs=[rows(d), rows(DN_WIDTH), rows(DN_WIDTH), rows(DN_WIDTH), rows(SG_WIDTH),
                  full(1, DN_HEAD_DIM), of_layer(DN_WIDTH + SG_WIDTH, d), full(1, d), of_layer(d, f),
                  of_layer(d, f), of_layer(f, d), full(1, d)],
        out_specs=rows(d),
        out_shape=jax.ShapeDtypeStruct((t, d), F32),
        compiler_params=pltpu.CompilerParams(dimension_semantics=("arbitrary",),
                                             vmem_limit_bytes=VMEM_LIMIT_BYTES),
        name="mix_ffn",
    )(x2, o_f, o_b, z, yb, dng, wo, fg, wg, wu, wd, fin)


def _pad_lanes(v):
    flat = v.reshape(1, -1).astype(F32)
    return jnp.pad(flat, ((0, 0), (0, LANES - flat.shape[1])))


def kernel(x, mix_norm_g, w_in, conv_w, dn_a_log, dn_dt_bias, dn_norm_g, sg_ln_g, sg_ln_b, sg_w, sg_b,
           sg_out_g, w_out, ffn_norm_g, w_gate, w_up, w_down, final_norm_g):
    bsz, seq, d = x.shape
    depth = w_in.shape[0]
    assert seq % TOKEN_TILE == 0 and seq % INPROJ_TILE == 0 and seq % DN_CHUNK == 0
    assert (INPROJ_TILE // INPROJ_SUBTILES) % SG_CHUNK == 0
    assert bsz % DN_BATCH_ROWS == 0 and seq % (DN_STEP_CHUNKS * DN_CHUNK) == 0
    x2 = x.reshape(bsz * seq, d)
    row = lambda v: v.reshape(1, -1).astype(F32)
    w_in_b = w_in.astype(BF16)
    wsg_b = w_in_b[:, :, OFF_SG:]
    sg_w_b = sg_w.astype(BF16)
    w_out_b, w_gate_b, w_up_b, w_down_b = (w.astype(BF16) for w in (w_out, w_gate, w_up, w_down))
    for l in range(depth):
        qkv, kt, z, gates, yb = _inproj(x2, row(mix_norm_g[l]), w_in_b, wsg_b, row(sg_ln_g[l]),
                                        row(sg_ln_b[l]), sg_w_b, sg_b[l].T.astype(F32),
                                        row(sg_out_g[l]), conv_w[l].astype(F32), seq, l)
        o_f, o_b = _delta_rule(qkv.reshape(bsz, seq, OFF_Z),
                               kt.reshape(bsz, seq // DN_CHUNK, DN_WIDTH, DN_CHUNK),
                               gates.reshape(bsz, seq, LANES),
                               _pad_lanes(dn_a_log[l]), _pad_lanes(dn_dt_bias[l]))
        x2 = _mix_ffn(x2, o_f.reshape(bsz * seq, DN_WIDTH), o_b.reshape(bsz * seq, DN_WIDTH), z, yb,
                      row(dn_norm_g[l]), w_out_b, row(ffn_norm_g[l]), w_gate_b, w_up_b, w_down_b,
                      row(final_norm_g), final=(l == depth - 1), layer=l)
    return x2.reshape(bsz, seq, d)
```

```python
import functools

import jax
import jax.numpy as jnp
from jax import lax
from jax.experimental import pallas as pl
from jax.experimental.pallas import tpu as pltpu

F32 = jnp.float32
BF16 = jnp.bfloat16

DN_HEADS = 4
DN_HEAD_DIM = 128
DN_WIDTH = DN_HEADS * DN_HEAD_DIM
SG_GROUPS = 4
SG_GROUP_DIM = 128
SG_WIDTH = SG_GROUPS * SG_GROUP_DIM
SG_CHUNK = 128
EPS = 1e-6

OFF_Z = 3 * DN_WIDTH
OFF_A = 4 * DN_WIDTH
OFF_B = OFF_A + 2 * DN_HEADS
OFF_SG = OFF_B + 2 * DN_HEADS

LANES = 128
F32_SUBLANES = 8
VMEM_LIMIT_BYTES = 56 * 1024 * 1024

DN_CHUNK = 128
NEUMANN_BLOCK = 8
DN_BATCH_ROWS = 4
DN_STEP_CHUNKS = 2
TOKEN_TILE = 512
INPROJ_TILE = 1024
INPROJ_SUBTILES = 4


def _rms(x, gain):
    return x * lax.rsqrt(jnp.mean(x * x, axis=-1, keepdims=True) + EPS) * gain


def _silu(x):
    h = 0.5 * x
    return h + h * jnp.tanh(h)


def _gelu_tanh(x):
    c = 0.7978845608028654
    h = 0.5 * x
    return h + h * jnp.tanh(x * (c + (c * 0.044715) * (x * x)))


def _bdot(a, b):
    return jnp.dot(a.astype(BF16), b.astype(BF16), preferred_element_type=F32)


def _conv_silu_norm(pre, prev_row, next_row, cw_ref, qkv_ref, kt_ref, row0):
    n, width = pre.shape
    sub = F32_SUBLANES
    first = lax.broadcasted_iota(jnp.int32, (sub, width), 0) == 0
    last = lax.broadcasted_iota(jnp.int32, (sub, width), 0) == sub - 1
    x_prev = pltpu.roll(pre, 1, 0)
    x_prev = jnp.concatenate([jnp.where(first, prev_row, x_prev[:sub]), x_prev[sub:]], axis=0)
    x_next = pltpu.roll(pre, n - 1, 0)
    x_next = jnp.concatenate([x_next[:n - sub], jnp.where(last, next_row, x_next[n - sub:])], axis=0)
    qkv = _silu(cw_ref[0:1, :] * x_prev + cw_ref[1:2, :] * pre + cw_ref[2:3, :] * x_next)
    rows = slice(row0, row0 + n)
    for h in range(2 * DN_HEADS):
        cols = slice(h * DN_HEAD_DIM, (h + 1) * DN_HEAD_DIM)
        t = qkv[:, cols]
        scale = DN_HEAD_DIM ** -0.5 if h < DN_HEADS else 1.0
        t = t * (lax.rsqrt(jnp.sum(t * t, axis=-1, keepdims=True) + EPS) * scale)
        qkv_ref[rows, cols] = t.astype(BF16)
        if h >= DN_HEADS:
            kcols = slice((h - DN_HEADS) * DN_HEAD_DIM, (h - DN_HEADS + 1) * DN_HEAD_DIM)
            for c in range(n // DN_CHUNK):
                kt_ref[(row0 + c * DN_CHUNK) // DN_CHUNK, kcols, :] = (
                    t[c * DN_CHUNK:(c + 1) * DN_CHUNK, :].T.astype(BF16))
    qkv_ref[rows, 2 * DN_WIDTH:] = qkv[:, 2 * DN_WIDTH:].astype(BF16)


def _spatial_gating(ps, lng_ref, lnb_ref, ws_ref, bst_ref, og_ref, yb_ref, row0):
    ps = _gelu_tanh(ps)
    u = ps[:, :SG_WIDTH]
    v = ps[:, SG_WIDTH:]
    vc = v - jnp.mean(v, axis=-1, keepdims=True)
    v = vc * lax.rsqrt(jnp.mean(vc * vc, axis=-1, keepdims=True) + EPS) * lng_ref[...] + lnb_ref[...]
    vb = v.astype(BF16)
    for n in range(ps.shape[0] // SG_CHUNK):
        rows = slice(n * SG_CHUNK, (n + 1) * SG_CHUNK)
        for g in range(SG_GROUPS):
            cols = slice(g * SG_GROUP_DIM, (g + 1) * SG_GROUP_DIM)
            mixed = jnp.dot(ws_ref[g], vb[rows, cols], preferred_element_type=F32) + bst_ref[:, g:g + 1]
            y = _rms(u[rows, cols] * mixed, og_ref[:, cols])
            yb_ref[row0 + n * SG_CHUNK:row0 + (n + 1) * SG_CHUNK, cols] = y.astype(BF16)


N_INPROJ_INPUTS = 13
N_INPROJ_OUTPUTS = 5


def _inproj_kernel(*refs, tiles_per_seq, n_casts):
    (x_ref, xprev_ref, xnext_ref, ng_ref, wqkvz_ref, wab_ref, wsg_ref, lng_ref, lnb_ref, ws_ref, bst_ref,
     og_ref, cw_ref) = refs[:N_INPROJ_INPUTS]
    cast_src_refs = refs[N_INPROJ_INPUTS:N_INPROJ_INPUTS + n_casts]
    outs = refs[N_INPROJ_INPUTS + n_casts:]
    qkv_ref, kt_ref, z_ref, gates_ref, yb_ref = outs[:N_INPROJ_OUTPUTS]
    for src_ref, dst_ref in zip(cast_src_refs, outs[N_INPROJ_OUTPUTS:]):
        dst_ref[...] = src_ref[...].astype(dst_ref.dtype)
    tm = x_ref.shape[0]
    st = tm // INPROJ_SUBTILES
    halo = xprev_ref.shape[0]
    sub_rows = [slice(s * st, (s + 1) * st) for s in range(INPROJ_SUBTILES)]
    h_halo = _rms(jnp.concatenate([xprev_ref[...], xnext_ref[...]], axis=0), ng_ref[...]).astype(BF16)
    hs = [_rms(x_ref[r, :], ng_ref[...]).astype(BF16) for r in sub_rows]
    pq0 = jnp.dot(jnp.concatenate([hs[0], h_halo], axis=0), wqkvz_ref[...], preferred_element_type=F32)
    pqs = [pq0[:st]] + [jnp.dot(h, wqkvz_ref[...], preferred_element_type=F32) for h in hs[1:]]
    pss = [jnp.dot(h, wsg_ref[...], preferred_element_type=F32) for h in hs]
    gates_ref[...] = jnp.dot(jnp.concatenate(hs, axis=0), wab_ref[...],
                             preferred_element_type=F32)

    pos = pl.program_id(0) % tiles_per_seq
    tile_prev = pq0[st + halo - 1:st + halo, :OFF_Z] * (pos > 0).astype(F32)
    tile_next = pq0[st + halo:st + halo + 1, :OFF_Z] * (pos < tiles_per_seq - 1).astype(F32)
    for s, r in enumerate(sub_rows):
        z_ref[r, :] = pqs[s][:, OFF_Z:].astype(BF16)
        prev_row = tile_prev if s == 0 else pqs[s - 1][st - 1:st, :OFF_Z]
        next_row = tile_next if s == INPROJ_SUBTILES - 1 else pqs[s + 1][0:1, :OFF_Z]
        _conv_silu_norm(pqs[s][:, :OFF_Z], prev_row, next_row, cw_ref, qkv_ref, kt_ref, s * st)
    for s in range(INPROJ_SUBTILES):
        _spatial_gating(pss[s], lng_ref, lnb_ref, ws_ref, bst_ref, og_ref, yb_ref, s * st)


def _inproj(x2, ng, w_in, wsg, lng, lnb, ws, bst, og, conv_w, seq, layer, casts=()):
    t, d = x2.shape
    tm = INPROJ_TILE
    steps = t // tm
    slab = lambda a: pl.BlockSpec((a.shape[0] // steps, a.shape[1]), lambda i: (i, 0))
    assert all(a.shape[0] % (steps * 2 * F32_SUBLANES) == 0 for a in casts)
    halo = F32_SUBLANES
    per_tile = tm // halo
    n_halo = t // halo
    once = pl.Buffered(1)
    full = lambda *shape: pl.BlockSpec(shape, lambda i: (0,) * len(shape), pipeline_mode=once)
    layer_cols = lambda width, col_block: pl.BlockSpec((None, d, width), lambda i: (layer, 0, col_block),
                                                       pipeline_mode=once)
    rows = lambda w: pl.BlockSpec((tm, w), lambda i: (i, 0))
    prev_spec = pl.BlockSpec((halo, d), lambda i: (jnp.maximum(i * per_tile - 1, 0), 0))
    next_spec = pl.BlockSpec((halo, d), lambda i: (jnp.minimum((i + 1) * per_tile, n_halo - 1), 0))
    return pl.pallas_call(
        functools.partial(_inproj_kernel, tiles_per_seq=seq // tm, n_casts=len(casts)),
        grid=(steps,),
        in_specs=[rows(d), prev_spec, next_spec, full(1, d), layer_cols(OFF_A, 0),
                  layer_cols(LANES, OFF_A // LANES), layer_cols(2 * SG_WIDTH, 0),
                  full(1, SG_WIDTH), full(1, SG_WIDTH),
                  pl.BlockSpec((None, SG_GROUPS, SG_CHUNK, SG_CHUNK), lambda i: (layer, 0, 0, 0),
                               pipeline_mode=once),
                  full(SG_CHUNK, SG_GROUPS), full(1, SG_WIDTH), full(3, OFF_Z)] + [slab(a) for a in casts],
        out_specs=[rows(OFF_Z), pl.BlockSpec((tm // DN_CHUNK, DN_WIDTH, DN_CHUNK), lambda i: (i, 0, 0)),
                   rows(DN_WIDTH), rows(LANES), rows(SG_WIDTH)] + [slab(a) for a in casts],
        out_shape=[jax.ShapeDtypeStruct((t, OFF_Z), BF16),
                   jax.ShapeDtypeStruct((t // DN_CHUNK, DN_WIDTH, DN_CHUNK), BF16),
                   jax.ShapeDtypeStruct((t, DN_WIDTH), BF16),
                   jax.ShapeDtypeStruct((t, LANES), F32), jax.ShapeDtypeStruct((t, SG_WIDTH), BF16)]
        + [jax.ShapeDtypeStruct(a.shape, BF16) for a in casts],
        compiler_params=pltpu.CompilerParams(dimension_semantics=("arbitrary",),
                                             vmem_limit_bytes=VMEM_LIMIT_BYTES),
        name="inproj_sg",
    )(x2, x2, x2, ng, w_in, w_in, wsg, lng, lnb, ws, bst, og, conv_w, *casts)


def _blockdiag(p):
    n = p.shape[0]
    zero = jnp.zeros((n, n), p.dtype)
    return jnp.concatenate([jnp.concatenate([p[:, :n], zero], axis=1),
                            jnp.concatenate([zero, p[:, n:]], axis=1)], axis=0)


def _pdot(p, q):
    return jnp.dot(p.astype(BF16), _blockdiag(q.astype(BF16)), preferred_element_type=F32)


def _unit_tri_inverse(a2s):
    n = a2s[0].shape[0]
    row = lax.broadcasted_iota(jnp.int32, (n, 2 * n), 0)
    colm = lax.broadcasted_iota(jnp.int32, (n, 2 * n), 1) % n
    eye = (row == colm).astype(F32)
    base = NEUMANN_BLOCK
    base_mask = (row // base) == (colm // base)
    ld = [jnp.where(base_mask, a2, 0.0) for a2 in a2s]
    ld2 = [_pdot(x, x) for x in ld]
    t = [eye - x for x in ld]
    both = [_pdot(jnp.concatenate([x, ti], axis=0), x) for x, ti in zip(ld2, t)]
    ld4 = [bo[:n] for bo in both]
    t = [ti + bo[n:] for ti, bo in zip(t, both)]
    t = [ti + _pdot(ti, x) for ti, x in zip(t, ld4)]

    levels = []
    m = base
    while m < n:
        levels.append(((row // (2 * m)) == (colm // (2 * m))) & ((row // m) != (colm // m)))
        m *= 2
    if not levels:
        return t
    a_t = [_pdot(jnp.concatenate([jnp.where(joins, a2, 0.0) for joins in levels], axis=0), ti)
           for a2, ti in zip(a2s, t)]
    w = [[at[j * n:(j + 1) * n] for j in range(len(levels))] for at in a_t]
    for lvl in range(len(levels)):
        x = [wi[lvl] for wi in w]
        prod = [_pdot(jnp.concatenate([ti] + wi[lvl + 1:], axis=0), xi) for ti, wi, xi in zip(t, w, x)]
        t = [ti - pr[:n] for ti, pr in zip(t, prod)]
        w = [wi[:lvl + 1] + [wj - pr[(j + 1) * n:(j + 2) * n] for j, wj in enumerate(wi[lvl + 1:])]
             for wi, pr in zip(w, prod)]
    return t


def _chunk_cumsum(x, rows, reverse):
    n = x.shape[0]
    s = 1
    while s < n:
        if reverse:
            x = x + jnp.where(rows < n - s, pltpu.roll(x, n - s, 0), 0.0)
        else:
            x = x + jnp.where(rows >= s, pltpu.roll(x, s, 0), 0.0)
        s *= 2
    return x


def _delta_operands(reverse, qkv, kt_all, gates, alog_ref, dtb_ref):
    c = DN_CHUNK
    row = lax.broadcasted_iota(jnp.int32, (c, c), 0)
    col = lax.broadcasted_iota(jnp.int32, (c, c), 1)
    mxu_dtype = qkv.dtype

    z = gates + dtb_ref[...]
    softplus = jnp.maximum(z, 0.0) + jnp.log(1.0 + jnp.exp(-jnp.abs(z)))
    g = -jnp.exp(alog_ref[...]) * softplus
    beta = jax.nn.sigmoid(gates)
    gc = _chunk_cumsum(g, row, reverse)
    gct = gc.T
    g_last = gc[0:1, :] if reverse else gc[c - 1:c, :]
    g_last_t = gct[:, 0:1] if reverse else gct[:, c - 1:c]
    e_gc = jnp.exp(gc)
    e_tail_t = jnp.exp(g_last_t - gct).astype(mxu_dtype)
    e_last = jnp.exp(g_last)

    incl = (row <= col) if reverse else (row >= col)
    strict = (row < col) if reverse else (row > col)
    d = 1 if reverse else 0

    chains = []
    for h in range(DN_HEADS):
        r = d * DN_HEADS + h
        q = qkv[:, h * DN_HEAD_DIM:(h + 1) * DN_HEAD_DIM]
        k = qkv[:, DN_WIDTH + h * DN_HEAD_DIM:DN_WIDTH + (h + 1) * DN_HEAD_DIM]
        v = qkv[:, 2 * DN_WIDTH + h * DN_HEAD_DIM:2 * DN_WIDTH + (h + 1) * DN_HEAD_DIM]
        kt = kt_all[h * DN_HEAD_DIM:(h + 1) * DN_HEAD_DIM, :]
        b_bc = jnp.broadcast_to(beta[:, 2 * DN_HEADS + r:2 * DN_HEADS + r + 1],
                                (c, DN_HEAD_DIM)).astype(mxu_dtype)
        eg_bc = jnp.broadcast_to(e_gc[:, r:r + 1], (c, DN_HEAD_DIM)).astype(mxu_dtype)
        kb = k * b_bc
        chains.append(dict(
            reverse=reverse, strict=strict, kt=kt,
            gram_lhs=jnp.concatenate([kb, q], axis=0),
            decay=jnp.exp(jnp.where(incl, gc[:, r:r + 1] - gct[r:r + 1, :], -1e30)),
            rhs=jnp.concatenate([v * b_bc, kb * eg_bc], axis=1),
            qdec=q * eg_bc, ktail=kt * e_tail_t[r:r + 1, :],
            e_last=jnp.broadcast_to(e_last[:, r:r + 1], (1, DN_HEAD_DIM))))
    return chains


def _wy_solve(a_list, rhs_list, rev_list):
    c = DN_CHUNK
    hc = c // 2
    lane = lax.broadcasted_iota(jnp.int32, (hc, c), 1)
    diag = [jnp.where(lane < hc, a[:hc], a[hc:]) for a in a_list]
    t12 = _unit_tri_inverse(diag)
    t21 = [pltpu.roll(t, hc, 1) for t in t12]
    t_first = [(t21 if rev else t12)[i][:, :hc] for i, rev in enumerate(rev_list)]
    t_second = [(t12 if rev else t21)[i][:, :hc] for i, rev in enumerate(rev_list)]
    a_off = [pltpu.roll(a[:hc], hc, 1)[:, :hc] if rev else a[hc:, :hc] for a, rev in zip(a_list, rev_list)]
    r_first = [r_[hc:] if rev else r_[:hc] for r_, rev in zip(rhs_list, rev_list)]
    r_second = [r_[:hc] if rev else r_[hc:] for r_, rev in zip(rhs_list, rev_list)]
    x_first = [_bdot(t, r_) for t, r_ in zip(t_first, r_first)]
    y = [_bdot(a, x) for a, x in zip(a_off, x_first)]
    x_second = [_bdot(t, r_ - y_) for t, r_, y_ in zip(t_second, r_second, y)]
    return [jnp.concatenate([x2, x1] if rev else [x1, x2], axis=0)
            for x1, x2, rev in zip(x_first, x_second, rev_list)]


def _delta_independent(chains):
    c = DN_CHUNK
    grams = [_bdot(p["gram_lhs"], p["kt"]) for p in chains]
    a_s = [jnp.where(p["strict"], gm[:c] * p["decay"], 0.0) for p, gm in zip(chains, grams)]
    attns = [gm[c:] * p["decay"] for p, gm in zip(chains, grams)]
    uw = _wy_solve(a_s, [p["rhs"] for p in chains], [p["reverse"] for p in chains])
    return [dict(u=x[:, :DN_HEAD_DIM],
                 wq=jnp.concatenate([x[:, DN_HEAD_DIM:].astype(BF16), p["qdec"].astype(BF16)], axis=0),
                 ak=jnp.concatenate([at.astype(BF16), p["ktail"].astype(BF16)], axis=0),
                 e_last=p["e_last"])
            for p, x, at in zip(chains, uw, attns)]


def _delta_recurrent(parts, states):
    c = DN_CHUNK
    rss = [_bdot(p["wq"], s_) for p, s_ in zip(parts, states)]
    v_news = [p["u"] - rs[:c] for p, rs in zip(parts, rss)]
    rvs = [_bdot(p["ak"], vn) for p, vn in zip(parts, v_news)]
    outs = [rs[c:] + rv[:c] for rs, rv in zip(rss, rvs)]
    new_states = [s_ * p["e_last"] + rv[c:] for p, s_, rv in zip(parts, states, rvs)]
    return outs, new_states


def _delta_kernel(xf_ref, xb_ref, ktf_ref, ktb_ref, gf_ref, gb_ref, alog_ref, dtb_ref, of_ref, ob_ref, s_ref):
    n_rows = xf_ref.shape[0]
    c = DN_CHUNK
    step_chunks = xf_ref.shape[1] // c

    @pl.when(pl.program_id(1) == 0)
    def _():
        s_ref[...] = jnp.zeros_like(s_ref)

    chains, places = [], []
    for j in range(step_chunks):
        jb = step_chunks - 1 - j
        for b in range(n_rows):
            chains += _delta_operands(False, xf_ref[b, j * c:(j + 1) * c, :], ktf_ref[b, j],
                                      gf_ref[b, j * c:(j + 1) * c, :], alog_ref, dtb_ref)
            places += [(of_ref, b, j, h) for h in range(DN_HEADS)]
            chains += _delta_operands(True, xb_ref[b, jb * c:(jb + 1) * c, :], ktb_ref[b, jb],
                                      gb_ref[b, jb * c:(jb + 1) * c, :], alog_ref, dtb_ref)
            places += [(ob_ref, b, jb, h) for h in range(DN_HEADS)]
    parts = _delta_independent(chains)
    per_chunk = len(chains) // step_chunks
    states = [s_ref[i] for i in range(per_chunk)]
    for j in range(step_chunks):
        sel = slice(j * per_chunk, (j + 1) * per_chunk)
        outs, states = _delta_recurrent(parts[sel], states)
        for o, (o_ref, b, jj, h) in zip(outs, places[sel]):
            o_ref[b, jj * c:(jj + 1) * c, h * DN_HEAD_DIM:(h + 1) * DN_HEAD_DIM] = o
    for i, s_ in enumerate(states):
        s_ref[i] = s_


def _delta_rule(qkv, kt, gates, alog_row, dtb_row):
    bsz, seq, width = qkv.shape
    c = DN_CHUNK
    nb = DN_BATCH_ROWS
    sc = DN_STEP_CHUNKS
    n_steps = seq // (sc * c)
    fwd = lambda n: n
    bwd = lambda n: n_steps - 1 - n
    chunk_spec = lambda w, block_of: pl.BlockSpec((nb, sc * c, w), lambda b, n: (b, block_of(n), 0))
    kt_spec = lambda block_of: pl.BlockSpec((nb, sc, DN_WIDTH, c), lambda b, n: (b, block_of(n), 0, 0))
    full = lambda *shape: pl.BlockSpec(shape, lambda b, n: (0,) * len(shape))
    out_sds = jax.ShapeDtypeStruct((bsz, seq, DN_WIDTH), F32)
    return pl.pallas_call(
        _delta_kernel,
        grid=(bsz // nb, n_steps),
        in_specs=[chunk_spec(width, fwd), chunk_spec(width, bwd), kt_spec(fwd), kt_spec(bwd),
                  chunk_spec(LANES, fwd), chunk_spec(LANES, bwd), full(1, LANES), full(1, LANES)],
        out_specs=[chunk_spec(DN_WIDTH, fwd), chunk_spec(DN_WIDTH, bwd)],
        out_shape=[out_sds, out_sds],
        scratch_shapes=[pltpu.VMEM((2 * nb * DN_HEADS, DN_HEAD_DIM, DN_HEAD_DIM), F32)],
        compiler_params=pltpu.CompilerParams(dimension_semantics=("arbitrary", "arbitrary"),
                                             vmem_limit_bytes=VMEM_LIMIT_BYTES),
        name="delta_rule",
    )(qkv, qkv, kt, kt, gates, gates, alog_row, dtb_row)


def _mix_ffn_kernel(x_ref, of_ref, ob_ref, z_ref, yb_ref, dng_ref, wo_ref, fg_ref, wg_ref, wu_ref, wd_ref,
                    fin_ref, out_ref, *, final):
    x1 = x_ref[...] + jnp.dot(yb_ref[...], wo_ref[DN_WIDTH:, :], preferred_element_type=F32)
    o = of_ref[...] + ob_ref[...]
    z = z_ref[...].astype(F32)
    heads = []
    for h in range(DN_HEADS):
        cols = slice(h * DN_HEAD_DIM, (h + 1) * DN_HEAD_DIM)
        heads.append((_rms(o[:, cols], dng_ref[...]) * _silu(z[:, cols])).astype(BF16))
    x1 = x1 + jnp.dot(jnp.concatenate(heads, axis=1), wo_ref[:DN_WIDTH, :], preferred_element_type=F32)
    hb = _rms(x1, fg_ref[...]).astype(BF16)
    gate = jnp.dot(hb, wg_ref[...], preferred_element_type=F32)
    up = jnp.dot(hb, wu_ref[...], preferred_element_type=F32)
    hid = (_silu(gate) * up).astype(BF16)
    out = x1 + jnp.dot(hid, wd_ref[...], preferred_element_type=F32)
    if final:
        out = _rms(out, fin_ref[...])
    out_ref[...] = out


def _mix_ffn(x2, o_f, o_b, z, yb, dng, wo, fg, wg, wu, wd, fin, final, layer):
    t, d = x2.shape
    f = wg.shape[2]
    tm = TOKEN_TILE
    once = pl.Buffered(1)
    full = lambda *shape: pl.BlockSpec(shape, lambda i: (0,) * len(shape), pipeline_mode=once)
    of_layer = lambda r, c: pl.BlockSpec((None, r, c), lambda i: (layer, 0, 0), pipeline_mode=once)
    rows = lambda w: pl.BlockSpec((tm, w), lambda i: (i, 0))
    return pl.pallas_call(
        functools.partial(_mix_ffn_kernel, final=final),
        grid=(t // tm,),
        in_specs=[rows(d), rows(DN_WIDTH), rows(DN_WIDTH), rows(DN_WIDTH), rows(SG_WIDTH),
                  full(1, DN_HEAD_DIM), of_layer(DN_WIDTH + SG_WIDTH, d), full(1, d), of_layer(d, f),
                  of_layer(d, f), of_layer(f, d), full(1, d)],
        out_specs=rows(d),
        out_shape=jax.ShapeDtypeStruct((t, d), F32),
        compiler_params=pltpu.CompilerParams(dimension_semantics=("arbitrary",),
                                             vmem_limit_bytes=VMEM_LIMIT_BYTES),
        name="mix_ffn",
    )(x2, o_f, o_b, z, yb, dng, wo, fg, wg, wu, wd, fin)


def _pad_lanes(v):
    flat = v.reshape(1, -1).astype(F32)
    return jnp.pad(flat, ((0, 0), (0, LANES - flat.shape[1])))


def kernel(x, mix_norm_g, w_in, conv_w, dn_a_log, dn_dt_bias, dn_norm_g, sg_ln_g, sg_ln_b, sg_w, sg_b,
           sg_out_g, w_out, ffn_norm_g, w_gate, w_up, w_down, final_norm_g):
    bsz, seq, d = x.shape
    depth = w_in.shape[0]
    assert seq % TOKEN_TILE == 0 and seq % INPROJ_TILE == 0 and seq % DN_CHUNK == 0
    assert (INPROJ_TILE // INPROJ_SUBTILES) % SG_CHUNK == 0
    assert bsz % DN_BATCH_ROWS == 0 and seq % (DN_STEP_CHUNKS * DN_CHUNK) == 0
    x2 = x.reshape(bsz * seq, d)
    row = lambda v: v.reshape(1, -1).astype(F32)
    sg_w_b = sg_w.astype(BF16)
    w_in_first = w_in[:1].astype(BF16)
    later = (w_in[1:], w_out, w_gate, w_up, w_down)
    for l in range(depth):
        w_in_b, layer_in = (w_in_first, 0) if l == 0 else (w_in_rest, l - 1)
        outs = _inproj(x2, row(mix_norm_g[l]), w_in_b, w_in_b[:, :, OFF_SG:], row(sg_ln_g[l]),
                       row(sg_ln_b[l]), sg_w_b[l - layer_in:], sg_b[l].T.astype(F32), row(sg_out_g[l]),
                       conv_w[l].astype(F32), seq, layer_in,
                       casts=tuple(w.reshape(-1, w.shape[-1]) for w in later) if l == 0 else ())
        qkv, kt, z, gates, yb = outs[:N_INPROJ_OUTPUTS]
        if l == 0:
            w_in_rest, w_out_b, w_gate_b, w_up_b, w_down_b = (
                o.reshape(w.shape) for o, w in zip(outs[N_INPROJ_OUTPUTS:], later))
        o_f, o_b = _delta_rule(qkv.reshape(bsz, seq, OFF_Z),
                               kt.reshape(bsz, seq // DN_CHUNK, DN_WIDTH, DN_CHUNK),
                               gates.reshape(bsz, seq, LANES),
                               _pad_lanes(dn_a_log[l]), _pad_lanes(dn_dt_bias[l]))
        x2 = _mix_ffn(x2, o_f.reshape(bsz * seq, DN_WIDTH), o_b.reshape(bsz * seq, DN_WIDTH), z, yb,
                      row(dn_norm_g[l]), w_out_b, row(ffn_norm_g[l]), w_gate_b, w_up_b, w_down_b,
                      row(final_norm_g), final=(l == depth - 1), layer=l)
    return x2.reshape(bsz, seq, d)
```

```python
import functools

import jax
import jax.numpy as jnp
from jax import lax
from jax.experimental import pallas as pl
from jax.experimental.pallas import tpu as pltpu

F32 = jnp.float32
BF16 = jnp.bfloat16

DN_HEADS = 4
DN_HEAD_DIM = 128
DN_WIDTH = DN_HEADS * DN_HEAD_DIM
SG_GROUPS = 4
SG_GROUP_DIM = 128
SG_WIDTH = SG_GROUPS * SG_GROUP_DIM
SG_CHUNK = 128
EPS = 1e-6

OFF_Z = 3 * DN_WIDTH
OFF_A = 4 * DN_WIDTH
OFF_B = OFF_A + 2 * DN_HEADS
OFF_SG = OFF_B + 2 * DN_HEADS

LANES = 128
F32_SUBLANES = 8
VMEM_LIMIT_BYTES = 56 * 1024 * 1024

DN_CHUNK = 128
NEUMANN_BLOCK = 8
DN_BATCH_ROWS = 4
DN_STEP_CHUNKS = 2
TOKEN_TILE = 512
INPROJ_TILE = 1024
INPROJ_SUBTILES = 4


def _rms(x, gain):
    return x * lax.rsqrt(jnp.mean(x * x, axis=-1, keepdims=True) + EPS) * gain


def _silu(x):
    h = 0.5 * x
    return h + h * jnp.tanh(h)


def _gelu_tanh(x):
    c = 0.7978845608028654
    h = 0.5 * x
    return h + h * jnp.tanh(x * (c + (c * 0.044715) * (x * x)))


def _bdot(a, b):
    return jnp.dot(a.astype(BF16), b.astype(BF16), preferred_element_type=F32)


def _conv_silu_norm(pre, prev_row, next_row, cw_ref, qkv_ref, kt_ref, row0):
    n, width = pre.shape
    sub = F32_SUBLANES
    first = lax.broadcasted_iota(jnp.int32, (sub, width), 0) == 0
    last = lax.broadcasted_iota(jnp.int32, (sub, width), 0) == sub - 1
    x_prev = pltpu.roll(pre, 1, 0)
    x_prev = jnp.concatenate([jnp.where(first, prev_row, x_prev[:sub]), x_prev[sub:]], axis=0)
    x_next = pltpu.roll(pre, n - 1, 0)
    x_next = jnp.concatenate([x_next[:n - sub], jnp.where(last, next_row, x_next[n - sub:])], axis=0)
    qkv = _silu(cw_ref[0:1, :] * x_prev + cw_ref[1:2, :] * pre + cw_ref[2:3, :] * x_next)
    rows = slice(row0, row0 + n)
    for h in range(2 * DN_HEADS):
        cols = slice(h * DN_HEAD_DIM, (h + 1) * DN_HEAD_DIM)
        t = qkv[:, cols]
        scale = DN_HEAD_DIM ** -0.5 if h < DN_HEADS else 1.0
        t = t * (lax.rsqrt(jnp.sum(t * t, axis=-1, keepdims=True) + EPS) * scale)
        qkv_ref[rows, cols] = t.astype(BF16)
        if h >= DN_HEADS:
            kcols = slice((h - DN_HEADS) * DN_HEAD_DIM, (h - DN_HEADS + 1) * DN_HEAD_DIM)
            for c in range(n // DN_CHUNK):
                kt_ref[(row0 + c * DN_CHUNK) // DN_CHUNK, kcols, :] = (
                    t[c * DN_CHUNK:(c + 1) * DN_CHUNK, :].T.astype(BF16))
    qkv_ref[rows, 2 * DN_WIDTH:] = qkv[:, 2 * DN_WIDTH:].astype(BF16)


def _spatial_gating(ps, lng_ref, lnb_ref, ws_ref, bst_ref, og_ref, yb_ref, row0):
    ps = _gelu_tanh(ps)
    u = ps[:, :SG_WIDTH]
    v = ps[:, SG_WIDTH:]
    vc = v - jnp.mean(v, axis=-1, keepdims=True)
    v = vc * lax.rsqrt(jnp.mean(vc * vc, axis=-1, keepdims=True) + EPS) * lng_ref[...] + lnb_ref[...]
    vb = v.astype(BF16)
    for n in range(ps.shape[0] // SG_CHUNK):
        rows = slice(n * SG_CHUNK, (n + 1) * SG_CHUNK)
        for g in range(SG_GROUPS):
            cols = slice(g * SG_GROUP_DIM, (g + 1) * SG_GROUP_DIM)
            mixed = jnp.dot(ws_ref[g], vb[rows, cols], preferred_element_type=F32) + bst_ref[:, g:g + 1]
            y = _rms(u[rows, cols] * mixed, og_ref[:, cols])
            yb_ref[row0 + n * SG_CHUNK:row0 + (n + 1) * SG_CHUNK, cols] = y.astype(BF16)


N_INPROJ_INPUTS = 13
N_INPROJ_OUTPUTS = 5


def _inproj_kernel(*refs, tiles_per_seq, n_casts):
    (x_ref, xprev_ref, xnext_ref, ng_ref, wqkvz_ref, wab_ref, wsg_ref, lng_ref, lnb_ref, ws_ref, bst_ref,
     og_ref, cw_ref) = refs[:N_INPROJ_INPUTS]
    cast_src_refs = refs[N_INPROJ_INPUTS:N_INPROJ_INPUTS + n_casts]
    outs = refs[N_INPROJ_INPUTS + n_casts:]
    qkv_ref, kt_ref, z_ref, gates_ref, yb_ref = outs[:N_INPROJ_OUTPUTS]
    for src_ref, dst_ref in zip(cast_src_refs, outs[N_INPROJ_OUTPUTS:]):
        dst_ref[...] = src_ref[...].astype(dst_ref.dtype)
    tm = x_ref.shape[0]
    st = tm // INPROJ_SUBTILES
    halo = xprev_ref.shape[0]
    sub_rows = [slice(s * st, (s + 1) * st) for s in range(INPROJ_SUBTILES)]
    h_halo = _rms(jnp.concatenate([xprev_ref[...], xnext_ref[...]], axis=0), ng_ref[...]).astype(BF16)
    hs = [_rms(x_ref[r, :], ng_ref[...]).astype(BF16) for r in sub_rows]
    pq0 = jnp.dot(jnp.concatenate([hs[0], h_halo], axis=0), wqkvz_ref[...], preferred_element_type=F32)
    pqs = [pq0[:st]] + [jnp.dot(h, wqkvz_ref[...], preferred_element_type=F32) for h in hs[1:]]
    pss = [jnp.dot(h, wsg_ref[...], preferred_element_type=F32) for h in hs]
    gates_ref[...] = jnp.dot(jnp.concatenate(hs, axis=0), wab_ref[...],
                             preferred_element_type=F32)

    pos = pl.program_id(0) % tiles_per_seq
    tile_prev = pq0[st + halo - 1:st + halo, :OFF_Z] * (pos > 0).astype(F32)
    tile_next = pq0[st + halo:st + halo + 1, :OFF_Z] * (pos < tiles_per_seq - 1).astype(F32)
    for s, r in enumerate(sub_rows):
        z_ref[r, :] = pqs[s][:, OFF_Z:].astype(BF16)
        prev_row = tile_prev if s == 0 else pqs[s - 1][st - 1:st, :OFF_Z]
        next_row = tile_next if s == INPROJ_SUBTILES - 1 else pqs[s + 1][0:1, :OFF_Z]
        _conv_silu_norm(pqs[s][:, :OFF_Z], prev_row, next_row, cw_ref, qkv_ref, kt_ref, s * st)
    for s in range(INPROJ_SUBTILES):
        _spatial_gating(pss[s], lng_ref, lnb_ref, ws_ref, bst_ref, og_ref, yb_ref, s * st)


def _inproj(x2, ng, w_in, wsg, lng, lnb, ws, bst, og, conv_w, seq, layer, casts=()):
    t, d = x2.shape
    tm = INPROJ_TILE
    steps = t // tm
    slab = lambda a: pl.BlockSpec((a.shape[0] // steps, a.shape[1]), lambda i: (i, 0))
    assert all(a.shape[0] % (steps * 2 * F32_SUBLANES) == 0 for a in casts)
    halo = F32_SUBLANES
    per_tile = tm // halo
    n_halo = t // halo
    once = pl.Buffered(1)
    full = lambda *shape: pl.BlockSpec(shape, lambda i: (0,) * len(shape), pipeline_mode=once)
    layer_cols = lambda width, col_block: pl.BlockSpec((None, d, width), lambda i: (layer, 0, col_block),
                                                       pipeline_mode=once)
    rows = lambda w: pl.BlockSpec((tm, w), lambda i: (i, 0))
    prev_spec = pl.BlockSpec((halo, d), lambda i: (jnp.maximum(i * per_tile - 1, 0), 0))
    next_spec = pl.BlockSpec((halo, d), lambda i: (jnp.minimum((i + 1) * per_tile, n_halo - 1), 0))
    return pl.pallas_call(
        functools.partial(_inproj_kernel, tiles_per_seq=seq // tm, n_casts=len(casts)),
        grid=(steps,),
        in_specs=[rows(d), prev_spec, next_spec, full(1, d), layer_cols(OFF_A, 0),
                  layer_cols(LANES, OFF_A // LANES), layer_cols(2 * SG_WIDTH, 0),
                  full(1, SG_WIDTH), full(1, SG_WIDTH),
                  pl.BlockSpec((None, SG_GROUPS, SG_CHUNK, SG_CHUNK), lambda i: (layer, 0, 0, 0),
                               pipeline_mode=once),
                  full(SG_CHUNK, SG_GROUPS), full(1, SG_WIDTH), full(3, OFF_Z)] + [slab(a) for a in casts],
        out_specs=[rows(OFF_Z), pl.BlockSpec((tm // DN_CHUNK, DN_WIDTH, DN_CHUNK), lambda i: (i, 0, 0)),
                   rows(DN_WIDTH), rows(LANES), rows(SG_WIDTH)] + [slab(a) for a in casts],
        out_shape=[jax.ShapeDtypeStruct((t, OFF_Z), BF16),
                   jax.ShapeDtypeStruct((t // DN_CHUNK, DN_WIDTH, DN_CHUNK), BF16),
                   jax.ShapeDtypeStruct((t, DN_WIDTH), BF16),
                   jax.ShapeDtypeStruct((t, LANES), F32), jax.ShapeDtypeStruct((t, SG_WIDTH), BF16)]
        + [jax.ShapeDtypeStruct(a.shape, BF16) for a in casts],
        compiler_params=pltpu.CompilerParams(dimension_semantics=("arbitrary",),
                                             vmem_limit_bytes=VMEM_LIMIT_BYTES),
        name="inproj_sg",
    )(x2, x2, x2, ng, w_in, w_in, wsg, lng, lnb, ws, bst, og, conv_w, *casts)


def _blockdiag(p):
    n = p.shape[0]
    zero = jnp.zeros((n, n), p.dtype)
    return jnp.concatenate([jnp.concatenate([p[:, :n], zero], axis=1),
                            jnp.concatenate([zero, p[:, n:]], axis=1)], axis=0)


def _pdot(p, q):
    return jnp.dot(p.astype(BF16), _blockdiag(q.astype(BF16)), preferred_element_type=F32)


def _unit_tri_inverse(a2s):
    n = a2s[0].shape[0]
    row = lax.broadcasted_iota(jnp.int32, (n, 2 * n), 0)
    colm = lax.broadcasted_iota(jnp.int32, (n, 2 * n), 1) % n
    eye = (row == colm).astype(F32)
    base = NEUMANN_BLOCK
    base_mask = (row // base) == (colm // base)
    ld = [jnp.where(base_mask, a2, 0.0) for a2 in a2s]
    ld2 = [_pdot(x, x) for x in ld]
    t = [eye - x for x in ld]
    both = [_pdot(jnp.concatenate([x, ti], axis=0), x) for x, ti in zip(ld2, t)]
    ld4 = [bo[:n] for bo in both]
    t = [ti + bo[n:] for ti, bo in zip(t, both)]
    t = [ti + _pdot(ti, x) for ti, x in zip(t, ld4)]

    levels = []
    m = base
    while m < n:
        levels.append(((row // (2 * m)) == (colm // (2 * m))) & ((row // m) != (colm // m)))
        m *= 2
    if not levels:
        return t
    a_t = [_pdot(jnp.concatenate([jnp.where(joins, a2, 0.0) for joins in levels], axis=0), ti)
           for a2, ti in zip(a2s, t)]
    w = [[at[j * n:(j + 1) * n] for j in range(len(levels))] for at in a_t]
    for lvl in range(len(levels)):
        x = [wi[lvl] for wi in w]
        prod = [_pdot(jnp.concatenate([ti] + wi[lvl + 1:], axis=0), xi) for ti, wi, xi in zip(t, w, x)]
        t = [ti - pr[:n] for ti, pr in zip(t, prod)]
        w = [wi[:lvl + 1] + [wj - pr[(j + 1) * n:(j + 2) * n] for j, wj in enumerate(wi[lvl + 1:])]
             for wi, pr in zip(w, prod)]
    return t


def _chunk_cumsum(x, rows, reverse):
    n = x.shape[0]
    s = 1
    while s < n:
        if reverse:
            x = x + jnp.where(rows < n - s, pltpu.roll(x, n - s, 0), 0.0)
        else:
            x = x + jnp.where(rows >= s, pltpu.roll(x, s, 0), 0.0)
        s *= 2
    return x


def _delta_operands(reverse, qkv, kt_all, gates, alog_ref, dtb_ref):
    c = DN_CHUNK
    row = lax.broadcasted_iota(jnp.int32, (c, c), 0)
    col = lax.broadcasted_iota(jnp.int32, (c, c), 1)
    mxu_dtype = qkv.dtype

    z = gates + dtb_ref[...]
    softplus = jnp.maximum(z, 0.0) + jnp.log(1.0 + jnp.exp(-jnp.abs(z)))
    g = -jnp.exp(alog_ref[...]) * softplus
    beta = jax.nn.sigmoid(gates)
    gc = _chunk_cumsum(g, row, reverse)
    gct = gc.T
    g_last = gc[0:1, :] if reverse else gc[c - 1:c, :]
    g_last_t = gct[:, 0:1] if reverse else gct[:, c - 1:c]
    e_gc = jnp.exp(gc)
    e_tail_t = jnp.exp(g_last_t - gct).astype(mxu_dtype)
    e_last = jnp.exp(g_last)

    incl = (row <= col) if reverse else (row >= col)
    strict = (row < col) if reverse else (row > col)
    d = 1 if reverse else 0

    chains = []
    for h in range(DN_HEADS):
        r = d * DN_HEADS + h
        q = qkv[:, h * DN_HEAD_DIM:(h + 1) * DN_HEAD_DIM]
        k = qkv[:, DN_WIDTH + h * DN_HEAD_DIM:DN_WIDTH + (h + 1) * DN_HEAD_DIM]
        v = qkv[:, 2 * DN_WIDTH + h * DN_HEAD_DIM:2 * DN_WIDTH + (h + 1) * DN_HEAD_DIM]
        kt = kt_all[h * DN_HEAD_DIM:(h + 1) * DN_HEAD_DIM, :]
        b_bc = jnp.broadcast_to(beta[:, 2 * DN_HEADS + r:2 * DN_HEADS + r + 1],
                                (c, DN_HEAD_DIM)).astype(mxu_dtype)
        eg_bc = jnp.broadcast_to(e_gc[:, r:r + 1], (c, DN_HEAD_DIM)).astype(mxu_dtype)
        kb = k * b_bc
        chains.append(dict(
            reverse=reverse, strict=strict, kt=kt,
            gram_lhs=jnp.concatenate([kb, q], axis=0),
            decay=jnp.exp(jnp.where(incl, gc[:, r:r + 1] - gct[r:r + 1, :], -1e30)),
            rhs=jnp.concatenate([v * b_bc, kb * eg_bc], axis=1),
            qdec=q * eg_bc, ktail=kt * e_tail_t[r:r + 1, :],
            e_last=jnp.broadcast_to(e_last[:, r:r + 1], (1, DN_HEAD_DIM))))
    return chains


def _wy_solve(a_list, rhs_list, rev_list):
    c = DN_CHUNK
    hc = c // 2
    lane = lax.broadcasted_iota(jnp.int32, (hc, c), 1)
    diag = [jnp.where(lane < hc, a[:hc], a[hc:]) for a in a_list]
    t12 = _unit_tri_inverse(diag)
    t21 = [pltpu.roll(t, hc, 1) for t in t12]
    t_first = [(t21 if rev else t12)[i][:, :hc] for i, rev in enumerate(rev_list)]
    t_second = [(t12 if rev else t21)[i][:, :hc] for i, rev in enumerate(rev_list)]
    a_off = [pltpu.roll(a[:hc], hc, 1)[:, :hc] if rev else a[hc:, :hc] for a, rev in zip(a_list, rev_list)]
    r_first = [r_[hc:] if rev else r_[:hc] for r_, rev in zip(rhs_list, rev_list)]
    r_second = [r_[:hc] if rev else r_[hc:] for r_, rev in zip(rhs_list, rev_list)]
    x_first = [_bdot(t, r_) for t, r_ in zip(t_first, r_first)]
    y = [_bdot(a, x) for a, x in zip(a_off, x_first)]
    x_second = [_bdot(t, r_ - y_) for t, r_, y_ in zip(t_second, r_second, y)]
    return [jnp.concatenate([x2, x1] if rev else [x1, x2], axis=0)
            for x1, x2, rev in zip(x_first, x_second, rev_list)]


def _delta_independent(chains):
    c = DN_CHUNK
    grams = [_bdot(p["gram_lhs"], p["kt"]) for p in chains]
    a_s = [jnp.where(p["strict"], gm[:c] * p["decay"], 0.0) for p, gm in zip(chains, grams)]
    attns = [gm[c:] * p["decay"] for p, gm in zip(chains, grams)]
    uw = _wy_solve(a_s, [p["rhs"] for p in chains], [p["reverse"] for p in chains])
    return [dict(u=x[:, :DN_HEAD_DIM],
                 wq=jnp.concatenate([x[:, DN_HEAD_DIM:].astype(BF16), p["qdec"].astype(BF16)], axis=0),
                 ak=jnp.concatenate([at.astype(BF16), p["ktail"].astype(BF16)], axis=0),
                 e_last=p["e_last"])
            for p, x, at in zip(chains, uw, attns)]


def _delta_recurrent(parts, states):
    c = DN_CHUNK
    rss = [_bdot(p["wq"], s_) for p, s_ in zip(parts, states)]
    v_news = [p["u"] - rs[:c] for p, rs in zip(parts, rss)]
    rvs = [_bdot(p["ak"], vn) for p, vn in zip(parts, v_news)]
    outs = [rs[c:] + rv[:c] for rs, rv in zip(rss, rvs)]
    new_states = [s_ * p["e_last"] + rv[c:] for p, s_, rv in zip(parts, states, rvs)]
    return outs, new_states


def _delta_kernel(xf_ref, xb_ref, ktf_ref, ktb_ref, gf_ref, gb_ref, alog_ref, dtb_ref, of_ref, ob_ref, s_ref):
    n_rows = xf_ref.shape[0]
    c = DN_CHUNK
    step_chunks = xf_ref.shape[1] // c

    @pl.when(pl.program_id(1) == 0)
    def _():
        s_ref[...] = jnp.zeros_like(s_ref)

    chains, places = [], []
    for j in range(step_chunks):
        jb = step_chunks - 1 - j
        for b in range(n_rows):
            chains += _delta_operands(False, xf_ref[b, j * c:(j + 1) * c, :], ktf_ref[b, j],
                                      gf_ref[b, j * c:(j + 1) * c, :], alog_ref, dtb_ref)
            places += [(of_ref, b, j, h) for h in range(DN_HEADS)]
            chains += _delta_operands(True, xb_ref[b, jb * c:(jb + 1) * c, :], ktb_ref[b, jb],
                                      gb_ref[b, jb * c:(jb + 1) * c, :], alog_ref, dtb_ref)
            places += [(ob_ref, b, jb, h) for h in range(DN_HEADS)]
    parts = _delta_independent(chains)
    per_chunk = len(chains) // step_chunks
    states = [s_ref[i] for i in range(per_chunk)]
    for j in range(step_chunks):
        sel = slice(j * per_chunk, (j + 1) * per_chunk)
        outs, states = _delta_recurrent(parts[sel], states)
        for o, (o_ref, b, jj, h) in zip(outs, places[sel]):
            o_ref[b, jj * c:(jj + 1) * c, h * DN_HEAD_DIM:(h + 1) * DN_HEAD_DIM] = o
    for i, s_ in enumerate(states):
        s_ref[i] = s_


def _delta_rule(qkv, kt, gates, alog_row, dtb_row):
    bsz, seq, width = qkv.shape
    c = DN_CHUNK
    nb = DN_BATCH_ROWS
    sc = DN_STEP_CHUNKS
    n_steps = seq // (sc * c)
    fwd = lambda n: n
    bwd = lambda n: n_steps - 1 - n
    chunk_spec = lambda w, block_of: pl.BlockSpec((nb, sc * c, w), lambda b, n: (b, block_of(n), 0))
    kt_spec = lambda block_of: pl.BlockSpec((nb, sc, DN_WIDTH, c), lambda b, n: (b, block_of(n), 0, 0))
    full = lambda *shape: pl.BlockSpec(shape, lambda b, n: (0,) * len(shape))
    out_sds = jax.ShapeDtypeStruct((bsz, seq, DN_WIDTH), F32)
    return pl.pallas_call(
        _delta_kernel,
        grid=(bsz // nb, n_steps),
        in_specs=[chunk_spec(width, fwd), chunk_spec(width, bwd), kt_spec(fwd), kt_spec(bwd),
                  chunk_spec(LANES, fwd), chunk_spec(LANES, bwd), full(1, LANES), full(1, LANES)],
        out_specs=[chunk_spec(DN_WIDTH, fwd), chunk_spec(DN_WIDTH, bwd)],
        out_shape=[out_sds, out_sds],
        scratch_shapes=[pltpu.VMEM((2 * nb * DN_HEADS, DN_HEAD_DIM, DN_HEAD_DIM), F32)],
        compiler_params=pltpu.CompilerParams(dimension_semantics=("arbitrary", "arbitrary"),
                                             vmem_limit_bytes=VMEM_LIMIT_BYTES),
        name="delta_rule",
    )(qkv, qkv, kt, kt, gates, gates, alog_row, dtb_row)


def _mix_ffn_kernel(x_ref, of_ref, ob_ref, z_ref, yb_ref, dng_ref, wo_ref, fg_ref, wg_ref, wu_ref, wd_ref,
                    fin_ref, out_ref, *, final):
    x1 = x_ref[...] + jnp.dot(yb_ref[...], wo_ref[DN_WIDTH:, :], preferred_element_type=F32)
    o = of_ref[...] + ob_ref[...]
    z = z_ref[...].astype(F32)
    heads = []
    for h in range(DN_HEADS):
        cols = slice(h * DN_HEAD_DIM, (h + 1) * DN_HEAD_DIM)
        heads.append((_rms(o[:, cols], dng_ref[...]) * _silu(z[:, cols])).astype(BF16))
    x1 = x1 + jnp.dot(jnp.concatenate(heads, axis=1), wo_ref[:DN_WIDTH, :], preferred_element_type=F32)
    hb = _rms(x1, fg_ref[...]).astype(BF16)
    gate = jnp.dot(hb, wg_ref[...], preferred_element_type=F32)
    up = jnp.dot(hb, wu_ref[...], preferred_element_type=F32)
    hid = (_silu(gate) * up).astype(BF16)
    out = x1 + jnp.dot(hid, wd_ref[...], preferred_element_type=F32)
    if final:
        out = _rms(out, fin_ref[...])
    out_ref[...] = out


def _mix_ffn(x2, o_f, o_b, z, yb, dng, wo, fg, wg, wu, wd, fin, final, layer):
    t, d = x2.shape
    f = wg.shape[2]
    tm = TOKEN_TILE
    once = pl.Buffered(1)
    full = lambda *shape: pl.BlockSpec(shape, lambda i: (0,) * len(shape), pipeline_mode=once)
    of_layer = lambda r, c: pl.BlockSpec((None, r, c), lambda i: (layer, 0, 0), pipeline_mode=once)
    rows = lambda w: pl.BlockSpec((tm, w), lambda i: (i, 0))
    return pl.pallas_call(
        functools.partial(_mix_ffn_kernel, final=final),
        grid=(t // tm,),
        in_specs=[rows(d), rows(DN_WIDTH), rows(DN_WIDTH), rows(DN_WIDTH), rows(SG_WIDTH),
                  full(1, DN_HEAD_DIM), of_layer(DN_WIDTH + SG_WIDTH, d), full(1, d), of_layer(d, f),
                  of_layer(d, f), of_layer(f, d), full(1, d)],
        out_specs=rows(d),
        out_shape=jax.ShapeDtypeStruct((t, d), F32),
        compiler_params=pltpu.CompilerParams(dimension_semantics=("arbitrary",),
                                             vmem_limit_bytes=VMEM_LIMIT_BYTES),
        name="mix_ffn",
    )(x2, o_f, o_b, z, yb, dng, wo, fg, wg, wu, wd, fin)


def _pad_lanes(v):
    flat = v.reshape(1, -1).astype(F32)
    return jnp.pad(flat, ((0, 0), (0, LANES - flat.shape[1])))


def kernel(x, mix_norm_g, w_in, conv_w, dn_a_log, dn_dt_bias, dn_norm_g, sg_ln_g, sg_ln_b, sg_w, sg_b,
           sg_out_g, w_out, ffn_norm_g, w_gate, w_up, w_down, final_norm_g):
    bsz, seq, d = x.shape
    depth = w_in.shape[0]
    assert seq % TOKEN_TILE == 0 and seq % INPROJ_TILE == 0 and seq % DN_CHUNK == 0
    assert (INPROJ_TILE // INPROJ_SUBTILES) % SG_CHUNK == 0
    assert bsz % DN_BATCH_ROWS == 0 and seq % (DN_STEP_CHUNKS * DN_CHUNK) == 0
    x2 = x.reshape(bsz * seq, d)
    row = lambda v: v.reshape(1, -1).astype(F32)
    sg_w_b = sg_w.astype(BF16)
    w_in_b = w_in.astype(BF16)
    wsg_b = w_in_b[:, :, OFF_SG:]
    later = (w_out, w_gate, w_up, w_down)
    for l in range(depth):
        outs = _inproj(x2, row(mix_norm_g[l]), w_in_b, wsg_b, row(sg_ln_g[l]), row(sg_ln_b[l]), sg_w_b,
                       sg_b[l].T.astype(F32), row(sg_out_g[l]), conv_w[l].astype(F32), seq, l,
                       casts=tuple(w.reshape(-1, w.shape[-1]) for w in later) if l == 0 else ())
        qkv, kt, z, gates, yb = outs[:N_INPROJ_OUTPUTS]
        if l == 0:
            w_out_b, w_gate_b, w_up_b, w_down_b = (
                o.reshape(w.shape) for o, w in zip(outs[N_INPROJ_OUTPUTS:], later))
        o_f, o_b = _delta_rule(qkv.reshape(bsz, seq, OFF_Z),
                               kt.reshape(bsz, seq // DN_CHUNK, DN_WIDTH, DN_CHUNK),
                               gates.reshape(bsz, seq, LANES),
                               _pad_lanes(dn_a_log[l]), _pad_lanes(dn_dt_bias[l]))
        x2 = _mix_ffn(x2, o_f.reshape(bsz * seq, DN_WIDTH), o_b.reshape(bsz * seq, DN_WIDTH), z, yb,
                      row(dn_norm_g[l]), w_out_b, row(ffn_norm_g[l]), w_gate_b, w_up_b, w_down_b,
                      row(final_norm_g), final=(l == depth - 1), layer=l)
    return x2.reshape(bsz, seq, d)
```

```python
import functools

import jax
import jax.numpy as jnp
from jax import lax
from jax.experimental import pallas as pl
from jax.experimental.pallas import tpu as pltpu

F32 = jnp.float32
BF16 = jnp.bfloat16

DN_HEADS = 4
DN_HEAD_DIM = 128
DN_WIDTH = DN_HEADS * DN_HEAD_DIM
SG_GROUPS = 4
SG_GROUP_DIM = 128
SG_WIDTH = SG_GROUPS * SG_GROUP_DIM
SG_CHUNK = 128
EPS = 1e-6

OFF_Z = 3 * DN_WIDTH
OFF_A = 4 * DN_WIDTH
OFF_B = OFF_A + 2 * DN_HEADS
OFF_SG = OFF_B + 2 * DN_HEADS

LANES = 128
F32_SUBLANES = 8
VMEM_LIMIT_BYTES = 60 * 1024 * 1024

DN_CHUNK = 128
NEUMANN_BLOCK = 8
DN_BATCH_ROWS = 4
DN_STEP_CHUNKS = 2
TOKEN_TILE = 1024
INPROJ_TILE = 1024
INPROJ_SUBTILES = 4


def _rms(x, gain):
    return x * lax.rsqrt(jnp.mean(x * x, axis=-1, keepdims=True) + EPS) * gain


def _silu(x):
    h = 0.5 * x
    return h + h * jnp.tanh(h)


def _gelu_tanh(x):
    c = 0.7978845608028654
    h = 0.5 * x
    return h + h * jnp.tanh(x * (c + (c * 0.044715) * (x * x)))


def _bdot(a, b):
    return jnp.dot(a.astype(BF16), b.astype(BF16), preferred_element_type=F32)


def _conv_silu_norm(pre, prev_row, next_row, cw_ref, qkv_ref, kt_ref, row0):
    n, width = pre.shape
    sub = F32_SUBLANES
    first = lax.broadcasted_iota(jnp.int32, (sub, width), 0) == 0
    last = lax.broadcasted_iota(jnp.int32, (sub, width), 0) == sub - 1
    x_prev = pltpu.roll(pre, 1, 0)
    x_prev = jnp.concatenate([jnp.where(first, prev_row, x_prev[:sub]), x_prev[sub:]], axis=0)
    x_next = pltpu.roll(pre, n - 1, 0)
    x_next = jnp.concatenate([x_next[:n - sub], jnp.where(last, next_row, x_next[n - sub:])], axis=0)
    qkv = _silu(cw_ref[0:1, :] * x_prev + cw_ref[1:2, :] * pre + cw_ref[2:3, :] * x_next)
    rows = slice(row0, row0 + n)
    for h in range(2 * DN_HEADS):
        cols = slice(h * DN_HEAD_DIM, (h + 1) * DN_HEAD_DIM)
        t = qkv[:, cols]
        scale = DN_HEAD_DIM ** -0.5 if h < DN_HEADS else 1.0
        t = t * (lax.rsqrt(jnp.sum(t * t, axis=-1, keepdims=True) + EPS) * scale)
        qkv_ref[rows, cols] = t.astype(BF16)
        if h >= DN_HEADS:
            kcols = slice((h - DN_HEADS) * DN_HEAD_DIM, (h - DN_HEADS + 1) * DN_HEAD_DIM)
            for c in range(n // DN_CHUNK):
                kt_ref[(row0 + c * DN_CHUNK) // DN_CHUNK, kcols, :] = (
                    t[c * DN_CHUNK:(c + 1) * DN_CHUNK, :].T.astype(BF16))
    qkv_ref[rows, 2 * DN_WIDTH:] = qkv[:, 2 * DN_WIDTH:].astype(BF16)


def _spatial_gating(ps, lng_ref, lnb_ref, ws_ref, bst_ref, og_ref, yb_ref, row0):
    ps = _gelu_tanh(ps)
    u = ps[:, :SG_WIDTH]
    v = ps[:, SG_WIDTH:]
    vc = v - jnp.mean(v, axis=-1, keepdims=True)
    v = vc * lax.rsqrt(jnp.mean(vc * vc, axis=-1, keepdims=True) + EPS) * lng_ref[...] + lnb_ref[...]
    vb = v.astype(BF16)
    for n in range(ps.shape[0] // SG_CHUNK):
        rows = slice(n * SG_CHUNK, (n + 1) * SG_CHUNK)
        for g in range(SG_GROUPS):
            cols = slice(g * SG_GROUP_DIM, (g + 1) * SG_GROUP_DIM)
            mixed = jnp.dot(ws_ref[g], vb[rows, cols], preferred_element_type=F32) + bst_ref[:, g:g + 1]
            y = _rms(u[rows, cols] * mixed, og_ref[:, cols])
            yb_ref[row0 + n * SG_CHUNK:row0 + (n + 1) * SG_CHUNK, cols] = y.astype(BF16)


N_INPROJ_INPUTS = 13
N_INPROJ_OUTPUTS = 5


def _inproj_kernel(*refs, tiles_per_seq, n_casts):
    (x_ref, xprev_ref, xnext_ref, ng_ref, wqkvz_ref, wab_ref, wsg_ref, lng_ref, lnb_ref, ws_ref, bst_ref,
     og_ref, cw_ref) = refs[:N_INPROJ_INPUTS]
    cast_src_refs = refs[N_INPROJ_INPUTS:N_INPROJ_INPUTS + n_casts]
    outs = refs[N_INPROJ_INPUTS + n_casts:]
    qkv_ref, kt_ref, z_ref, gates_ref, yb_ref = outs[:N_INPROJ_OUTPUTS]
    for src_ref, dst_ref in zip(cast_src_refs, outs[N_INPROJ_OUTPUTS:]):
        dst_ref[...] = src_ref[...].astype(dst_ref.dtype)
    tm = x_ref.shape[0]
    st = tm // INPROJ_SUBTILES
    halo = xprev_ref.shape[0]
    sub_rows = [slice(s * st, (s + 1) * st) for s in range(INPROJ_SUBTILES)]
    h_halo = _rms(jnp.concatenate([xprev_ref[...], xnext_ref[...]], axis=0), ng_ref[...]).astype(BF16)
    hs = [_rms(x_ref[r, :], ng_ref[...]).astype(BF16) for r in sub_rows]
    pq0 = jnp.dot(jnp.concatenate([hs[0], h_halo], axis=0), wqkvz_ref[...], preferred_element_type=F32)
    pqs = [pq0[:st]] + [jnp.dot(h, wqkvz_ref[...], preferred_element_type=F32) for h in hs[1:]]
    pss = [jnp.dot(h, wsg_ref[...], preferred_element_type=F32) for h in hs]
    gates_ref[...] = jnp.dot(jnp.concatenate(hs, axis=0), wab_ref[...],
                             preferred_element_type=F32)

    pos = pl.program_id(0) % tiles_per_seq
    tile_prev = pq0[st + halo - 1:st + halo, :OFF_Z] * (pos > 0).astype(F32)
    tile_next = pq0[st + halo:st + halo + 1, :OFF_Z] * (pos < tiles_per_seq - 1).astype(F32)
    for s, r in enumerate(sub_rows):
        z_ref[r, :] = pqs[s][:, OFF_Z:].astype(BF16)
        prev_row = tile_prev if s == 0 else pqs[s - 1][st - 1:st, :OFF_Z]
        next_row = tile_next if s == INPROJ_SUBTILES - 1 else pqs[s + 1][0:1, :OFF_Z]
        _conv_silu_norm(pqs[s][:, :OFF_Z], prev_row, next_row, cw_ref, qkv_ref, kt_ref, s * st)
    for s in range(INPROJ_SUBTILES):
        _spatial_gating(pss[s], lng_ref, lnb_ref, ws_ref, bst_ref, og_ref, yb_ref, s * st)


def _inproj(x2, ng, w_in, wsg, lng, lnb, ws, bst, og, conv_w, seq, layer, casts=()):
    t, d = x2.shape
    tm = INPROJ_TILE
    steps = t // tm
    slab = lambda a: pl.BlockSpec((a.shape[0] // steps, a.shape[1]), lambda i: (i, 0))
    assert all(a.shape[0] % (steps * 2 * F32_SUBLANES) == 0 for a in casts)
    halo = F32_SUBLANES
    per_tile = tm // halo
    n_halo = t // halo
    once = pl.Buffered(1)
    full = lambda *shape: pl.BlockSpec(shape, lambda i: (0,) * len(shape), pipeline_mode=once)
    layer_cols = lambda width, col_block: pl.BlockSpec((None, d, width), lambda i: (layer, 0, col_block),
                                                       pipeline_mode=once)
    rows = lambda w: pl.BlockSpec((tm, w), lambda i: (i, 0))
    prev_spec = pl.BlockSpec((halo, d), lambda i: (jnp.maximum(i * per_tile - 1, 0), 0))
    next_spec = pl.BlockSpec((halo, d), lambda i: (jnp.minimum((i + 1) * per_tile, n_halo - 1), 0))
    return pl.pallas_call(
        functools.partial(_inproj_kernel, tiles_per_seq=seq // tm, n_casts=len(casts)),
        grid=(steps,),
        in_specs=[rows(d), prev_spec, next_spec, full(1, d), layer_cols(OFF_A, 0),
                  layer_cols(LANES, OFF_A // LANES), layer_cols(2 * SG_WIDTH, 0),
                  full(1, SG_WIDTH), full(1, SG_WIDTH),
                  pl.BlockSpec((None, SG_GROUPS, SG_CHUNK, SG_CHUNK), lambda i: (layer, 0, 0, 0),
                               pipeline_mode=once),
                  full(SG_CHUNK, SG_GROUPS), full(1, SG_WIDTH), full(3, OFF_Z)] + [slab(a) for a in casts],
        out_specs=[rows(OFF_Z), pl.BlockSpec((tm // DN_CHUNK, DN_WIDTH, DN_CHUNK), lambda i: (i, 0, 0)),
                   rows(DN_WIDTH), rows(LANES), rows(SG_WIDTH)] + [slab(a) for a in casts],
        out_shape=[jax.ShapeDtypeStruct((t, OFF_Z), BF16),
                   jax.ShapeDtypeStruct((t // DN_CHUNK, DN_WIDTH, DN_CHUNK), BF16),
                   jax.ShapeDtypeStruct((t, DN_WIDTH), BF16),
                   jax.ShapeDtypeStruct((t, LANES), F32), jax.ShapeDtypeStruct((t, SG_WIDTH), BF16)]
        + [jax.ShapeDtypeStruct(a.shape, BF16) for a in casts],
        compiler_params=pltpu.CompilerParams(dimension_semantics=("arbitrary",),
                                             vmem_limit_bytes=VMEM_LIMIT_BYTES),
        name="inproj_sg",
    )(x2, x2, x2, ng, w_in, w_in, wsg, lng, lnb, ws, bst, og, conv_w, *casts)


def _blockdiag(p):
    n = p.shape[0]
    zero = jnp.zeros((n, n), p.dtype)
    return jnp.concatenate([jnp.concatenate([p[:, :n], zero], axis=1),
                            jnp.concatenate([zero, p[:, n:]], axis=1)], axis=0)


def _pdot(p, q):
    return jnp.dot(p.astype(BF16), _blockdiag(q.astype(BF16)), preferred_element_type=F32)


def _unit_tri_inverse(a2s):
    n = a2s[0].shape[0]
    row = lax.broadcasted_iota(jnp.int32, (n, 2 * n), 0)
    colm = lax.broadcasted_iota(jnp.int32, (n, 2 * n), 1) % n
    eye = (row == colm).astype(F32)
    base = NEUMANN_BLOCK
    base_mask = (row // base) == (colm // base)
    ld = [jnp.where(base_mask, a2, 0.0) for a2 in a2s]
    ld2 = [_pdot(x, x) for x in ld]
    t = [eye - x for x in ld]
    both = [_pdot(jnp.concatenate([x, ti], axis=0), x) for x, ti in zip(ld2, t)]
    ld4 = [bo[:n] for bo in both]
    t = [ti + bo[n:] for ti, bo in zip(t, both)]
    t = [ti + _pdot(ti, x) for ti, x in zip(t, ld4)]

    levels = []
    m = base
    while m < n:
        levels.append(((row // (2 * m)) == (colm // (2 * m))) & ((row // m) != (colm // m)))
        m *= 2
    if not levels:
        return t
    a_t = [_pdot(jnp.concatenate([jnp.where(joins, a2, 0.0) for joins in levels], axis=0), ti)
           for a2, ti in zip(a2s, t)]
    w = [[at[j * n:(j + 1) * n] for j in range(len(levels))] for at in a_t]
    for lvl in range(len(levels)):
        x = [wi[lvl] for wi in w]
        prod = [_pdot(jnp.concatenate([ti] + wi[lvl + 1:], axis=0), xi) for ti, wi, xi in zip(t, w, x)]
        t = [ti - pr[:n] for ti, pr in zip(t, prod)]
        w = [wi[:lvl + 1] + [wj - pr[(j + 1) * n:(j + 2) * n] for j, wj in enumerate(wi[lvl + 1:])]
             for wi, pr in zip(w, prod)]
    return t


def _chunk_cumsum(x, rows, reverse):
    n = x.shape[0]
    s = 1
    while s < n:
        if reverse:
            x = x + jnp.where(rows < n - s, pltpu.roll(x, n - s, 0), 0.0)
        else:
            x = x + jnp.where(rows >= s, pltpu.roll(x, s, 0), 0.0)
        s *= 2
    return x


def _delta_operands(reverse, qkv, kt_all, gates, alog_ref, dtb_ref):
    c = DN_CHUNK
    row = lax.broadcasted_iota(jnp.int32, (c, c), 0)
    col = lax.broadcasted_iota(jnp.int32, (c, c), 1)
    mxu_dtype = qkv.dtype

    z = gates + dtb_ref[...]
    softplus = jnp.maximum(z, 0.0) + jnp.log(1.0 + jnp.exp(-jnp.abs(z)))
    g = -jnp.exp(alog_ref[...]) * softplus
    beta = jax.nn.sigmoid(gates)
    gc = _chunk_cumsum(g, row, reverse)
    gct = gc.T
    g_last = gc[0:1, :] if reverse else gc[c - 1:c, :]
    g_last_t = gct[:, 0:1] if reverse else gct[:, c - 1:c]
    e_gc = jnp.exp(gc)
    e_tail_t = jnp.exp(g_last_t - gct).astype(mxu_dtype)
    e_last = jnp.exp(g_last)

    incl = (row <= col) if reverse else (row >= col)
    strict = (row < col) if reverse else (row > col)
    d = 1 if reverse else 0

    chains = []
    for h in range(DN_HEADS):
        r = d * DN_HEADS + h
        q = qkv[:, h * DN_HEAD_DIM:(h + 1) * DN_HEAD_DIM]
        k = qkv[:, DN_WIDTH + h * DN_HEAD_DIM:DN_WIDTH + (h + 1) * DN_HEAD_DIM]
        v = qkv[:, 2 * DN_WIDTH + h * DN_HEAD_DIM:2 * DN_WIDTH + (h + 1) * DN_HEAD_DIM]
        kt = kt_all[h * DN_HEAD_DIM:(h + 1) * DN_HEAD_DIM, :]
        b_bc = jnp.broadcast_to(beta[:, 2 * DN_HEADS + r:2 * DN_HEADS + r + 1],
                                (c, DN_HEAD_DIM)).astype(mxu_dtype)
        eg_bc = jnp.broadcast_to(e_gc[:, r:r + 1], (c, DN_HEAD_DIM)).astype(mxu_dtype)
        kb = k * b_bc
        chains.append(dict(
            reverse=reverse, strict=strict, kt=kt,
            gram_lhs=jnp.concatenate([kb, q], axis=0),
            decay=jnp.exp(jnp.where(incl, gc[:, r:r + 1] - gct[r:r + 1, :], -1e30)),
            rhs=jnp.concatenate([v * b_bc, kb * eg_bc], axis=1),
            qdec=q * eg_bc, ktail=kt * e_tail_t[r:r + 1, :],
            e_last=jnp.broadcast_to(e_last[:, r:r + 1], (1, DN_HEAD_DIM))))
    return chains


def _wy_solve(a_list, rhs_list, rev_list):
    c = DN_CHUNK
    hc = c // 2
    lane = lax.broadcasted_iota(jnp.int32, (hc, c), 1)
    diag = [jnp.where(lane < hc, a[:hc], a[hc:]) for a in a_list]
    t12 = _unit_tri_inverse(diag)
    t21 = [pltpu.roll(t, hc, 1) for t in t12]
    t_first = [(t21 if rev else t12)[i][:, :hc] for i, rev in enumerate(rev_list)]
    t_second = [(t12 if rev else t21)[i][:, :hc] for i, rev in enumerate(rev_list)]
    a_off = [pltpu.roll(a[:hc], hc, 1)[:, :hc] if rev else a[hc:, :hc] for a, rev in zip(a_list, rev_list)]
    r_first = [r_[hc:] if rev else r_[:hc] for r_, rev in zip(rhs_list, rev_list)]
    r_second = [r_[:hc] if rev else r_[hc:] for r_, rev in zip(rhs_list, rev_list)]
    x_first = [_bdot(t, r_) for t, r_ in zip(t_first, r_first)]
    y = [_bdot(a, x) for a, x in zip(a_off, x_first)]
    x_second = [_bdot(t, r_ - y_) for t, r_, y_ in zip(t_second, r_second, y)]
    return [jnp.concatenate([x2, x1] if rev else [x1, x2], axis=0)
            for x1, x2, rev in zip(x_first, x_second, rev_list)]


def _delta_independent(chains):
    c = DN_CHUNK
    grams = [_bdot(p["gram_lhs"], p["kt"]) for p in chains]
    a_s = [jnp.where(p["strict"], gm[:c] * p["decay"], 0.0) for p, gm in zip(chains, grams)]
    attns = [gm[c:] * p["decay"] for p, gm in zip(chains, grams)]
    uw = _wy_solve(a_s, [p["rhs"] for p in chains], [p["reverse"] for p in chains])
    return [dict(u=x[:, :DN_HEAD_DIM],
                 wq=jnp.concatenate([x[:, DN_HEAD_DIM:].astype(BF16), p["qdec"].astype(BF16)], axis=0),
                 ak=jnp.concatenate([at.astype(BF16), p["ktail"].astype(BF16)], axis=0),
                 e_last=p["e_last"])
            for p, x, at in zip(chains, uw, attns)]


def _delta_recurrent(parts, states):
    c = DN_CHUNK
    rss = [_bdot(p["wq"], s_) for p, s_ in zip(parts, states)]
    v_news = [p["u"] - rs[:c] for p, rs in zip(parts, rss)]
    rvs = [_bdot(p["ak"], vn) for p, vn in zip(parts, v_news)]
    outs = [rs[c:] + rv[:c] for rs, rv in zip(rss, rvs)]
    new_states = [s_ * p["e_last"] + rv[c:] for p, s_, rv in zip(parts, states, rvs)]
    return outs, new_states


def _delta_kernel(xf_ref, xb_ref, ktf_ref, ktb_ref, gf_ref, gb_ref, alog_ref, dtb_ref, of_ref, ob_ref, s_ref):
    n_rows = xf_ref.shape[0]
    c = DN_CHUNK
    step_chunks = xf_ref.shape[1] // c

    @pl.when(pl.program_id(1) == 0)
    def _():
        s_ref[...] = jnp.zeros_like(s_ref)

    chains, places = [], []
    for j in range(step_chunks):
        jb = step_chunks - 1 - j
        for b in range(n_rows):
            chains += _delta_operands(False, xf_ref[b, j * c:(j + 1) * c, :], ktf_ref[b, j],
                                      gf_ref[b, j * c:(j + 1) * c, :], alog_ref, dtb_ref)
            places += [(of_ref, b, j, h) for h in range(DN_HEADS)]
            chains += _delta_operands(True, xb_ref[b, jb * c:(jb + 1) * c, :], ktb_ref[b, jb],
                                      gb_ref[b, jb * c:(jb + 1) * c, :], alog_ref, dtb_ref)
            places += [(ob_ref, b, jb, h) for h in range(DN_HEADS)]
    parts = _delta_independent(chains)
    per_chunk = len(chains) // step_chunks
    states = [s_ref[i] for i in range(per_chunk)]
    for j in range(step_chunks):
        sel = slice(j * per_chunk, (j + 1) * per_chunk)
        outs, states = _delta_recurrent(parts[sel], states)
        for o, (o_ref, b, jj, h) in zip(outs, places[sel]):
            o_ref[b, jj * c:(jj + 1) * c, h * DN_HEAD_DIM:(h + 1) * DN_HEAD_DIM] = o.astype(o_ref.dtype)
    for i, s_ in enumerate(states):
        s_ref[i] = s_


def _delta_rule(qkv, kt, gates, alog_row, dtb_row):
    bsz, seq, width = qkv.shape
    c = DN_CHUNK
    nb = DN_BATCH_ROWS
    sc = DN_STEP_CHUNKS
    n_steps = seq // (sc * c)
    fwd = lambda n: n
    bwd = lambda n: n_steps - 1 - n
    chunk_spec = lambda w, block_of: pl.BlockSpec((nb, sc * c, w), lambda b, n: (b, block_of(n), 0))
    kt_spec = lambda block_of: pl.BlockSpec((nb, sc, DN_WIDTH, c), lambda b, n: (b, block_of(n), 0, 0))
    full = lambda *shape: pl.BlockSpec(shape, lambda b, n: (0,) * len(shape))
    out_sds = jax.ShapeDtypeStruct((bsz, seq, DN_WIDTH), BF16)
    return pl.pallas_call(
        _delta_kernel,
        grid=(bsz // nb, n_steps),
        in_specs=[chunk_spec(width, fwd), chunk_spec(width, bwd), kt_spec(fwd), kt_spec(bwd),
                  chunk_spec(LANES, fwd), chunk_spec(LANES, bwd), full(1, LANES), full(1, LANES)],
        out_specs=[chunk_spec(DN_WIDTH, fwd), chunk_spec(DN_WIDTH, bwd)],
        out_shape=[out_sds, out_sds],
        scratch_shapes=[pltpu.VMEM((2 * nb * DN_HEADS, DN_HEAD_DIM, DN_HEAD_DIM), F32)],
        compiler_params=pltpu.CompilerParams(dimension_semantics=("arbitrary", "arbitrary"),
                                             vmem_limit_bytes=VMEM_LIMIT_BYTES),
        name="delta_rule",
    )(qkv, qkv, kt, kt, gates, gates, alog_row, dtb_row)


def _mix_ffn_kernel(x_ref, of_ref, ob_ref, z_ref, yb_ref, dng_ref, wo_ref, fg_ref, wg_ref, wu_ref, wd_ref,
                    fin_ref, out_ref, *, final):
    x1 = x_ref[...] + jnp.dot(yb_ref[...], wo_ref[DN_WIDTH:, :], preferred_element_type=F32)
    o = of_ref[...].astype(F32) + ob_ref[...].astype(F32)
    z = z_ref[...].astype(F32)
    heads = []
    for h in range(DN_HEADS):
        cols = slice(h * DN_HEAD_DIM, (h + 1) * DN_HEAD_DIM)
        heads.append((_rms(o[:, cols], dng_ref[...]) * _silu(z[:, cols])).astype(BF16))
    x1 = x1 + jnp.dot(jnp.concatenate(heads, axis=1), wo_ref[:DN_WIDTH, :], preferred_element_type=F32)
    hb = _rms(x1, fg_ref[...]).astype(BF16)
    gate = jnp.dot(hb, wg_ref[...], preferred_element_type=F32)
    up = jnp.dot(hb, wu_ref[...], preferred_element_type=F32)
    hid = (_silu(gate) * up).astype(BF16)
    out = x1 + jnp.dot(hid, wd_ref[...], preferred_element_type=F32)
    if final:
        out = _rms(out, fin_ref[...])
    out_ref[...] = out


def _mix_ffn(x2, o_f, o_b, z, yb, dng, wo, fg, wg, wu, wd, fin, final, layer):
    t, d = x2.shape
    f = wg.shape[2]
    tm = TOKEN_TILE
    once = pl.Buffered(1)
    full = lambda *shape: pl.BlockSpec(shape, lambda i: (0,) * len(shape), pipeline_mode=once)
    of_layer = lambda r, c: pl.BlockSpec((None, r, c), lambda i: (layer, 0, 0), pipeline_mode=once)
    rows = lambda w: pl.BlockSpec((tm, w), lambda i: (i, 0))
    return pl.pallas_call(
        functools.partial(_mix_ffn_kernel, final=final),
        grid=(t // tm,),
        in_specs=[rows(d), rows(DN_WIDTH), rows(DN_WIDTH), rows(DN_WIDTH), rows(SG_WIDTH),
                  full(1, DN_HEAD_DIM), of_layer(DN_WIDTH + SG_WIDTH, d), full(1, d), of_layer(d, f),
                  of_layer(d, f), of_layer(f, d), full(1, d)],
        out_specs=rows(d),
        out_shape=jax.ShapeDtypeStruct((t, d), F32),
        compiler_params=pltpu.CompilerParams(dimension_semantics=("arbitrary",),
                                             vmem_limit_bytes=VMEM_LIMIT_BYTES),
        name="mix_ffn",
    )(x2, o_f, o_b, z, yb, dng, wo, fg, wg, wu, wd, fin)


def _pad_lanes(v):
    flat = v.reshape(1, -1).astype(F32)
    return jnp.pad(flat, ((0, 0), (0, LANES - flat.shape[1])))


def kernel(x, mix_norm_g, w_in, conv_w, dn_a_log, dn_dt_bias, dn_norm_g, sg_ln_g, sg_ln_b, sg_w, sg_b,
           sg_out_g, w_out, ffn_norm_g, w_gate, w_up, w_down, final_norm_g):
    bsz, seq, d = x.shape
    depth = w_in.shape[0]
    assert seq % TOKEN_TILE == 0 and seq % INPROJ_TILE == 0 and seq % DN_CHUNK == 0
    assert (INPROJ_TILE // INPROJ_SUBTILES) % SG_CHUNK == 0
    assert bsz % DN_BATCH_ROWS == 0 and seq % (DN_STEP_CHUNKS * DN_CHUNK) == 0
    x2 = x.reshape(bsz * seq, d)
    row = lambda v: v.reshape(1, -1).astype(F32)
    sg_w_b = sg_w.astype(BF16)
    w_in_b = w_in.astype(BF16)
    wsg_b = w_in_b[:, :, OFF_SG:]
    later = (w_out, w_gate, w_up, w_down)
    for l in range(depth):
        outs = _inproj(x2, row(mix_norm_g[l]), w_in_b, wsg_b, row(sg_ln_g[l]), row(sg_ln_b[l]), sg_w_b,
                       sg_b[l].T.astype(F32), row(sg_out_g[l]), conv_w[l].astype(F32), seq, l,
                       casts=tuple(w.reshape(-1, w.shape[-1]) for w in later) if l == 0 else ())
        qkv, kt, z, gates, yb = outs[:N_INPROJ_OUTPUTS]
        if l == 0:
            w_out_b, w_gate_b, w_up_b, w_down_b = (
                o.reshape(w.shape) for o, w in zip(outs[N_INPROJ_OUTPUTS:], later))
        o_f, o_b = _delta_rule(qkv.reshape(bsz, seq, OFF_Z),
                               kt.reshape(bsz, seq // DN_CHUNK, DN_WIDTH, DN_CHUNK),
                               gates.reshape(bsz, seq, LANES),
                               _pad_lanes(dn_a_log[l]), _pad_lanes(dn_dt_bias[l]))
        x2 = _mix_ffn(x2, o_f.reshape(bsz * seq, DN_WIDTH), o_b.reshape(bsz * seq, DN_WIDTH), z, yb,
                      row(dn_norm_g[l]), w_out_b, row(ffn_norm_g[l]), w_gate_b, w_up_b, w_down_b,
                      row(final_norm_g), final=(l == depth - 1), layer=l)
    return x2.reshape(bsz, seq, d)
```

```python
import functools

import jax
import jax.numpy as jnp
from jax import lax
from jax.experimental import pallas as pl
from jax.experimental.pallas import tpu as pltpu

F32 = jnp.float32
BF16 = jnp.bfloat16

DN_HEADS = 4
DN_HEAD_DIM = 128
DN_WIDTH = DN_HEADS * DN_HEAD_DIM
SG_GROUPS = 4
SG_GROUP_DIM = 128
SG_WIDTH = SG_GROUPS * SG_GROUP_DIM
SG_CHUNK = 128
EPS = 1e-6

OFF_Z = 3 * DN_WIDTH
OFF_A = 4 * DN_WIDTH
OFF_B = OFF_A + 2 * DN_HEADS
OFF_SG = OFF_B + 2 * DN_HEADS

LANES = 128
F32_SUBLANES = 8
VMEM_LIMIT_BYTES = 56 * 1024 * 1024

DN_CHUNK = 128
NEUMANN_BLOCK = 8
DN_BATCH_ROWS = 4
DN_STEP_CHUNKS = 2
TOKEN_TILE = 512
INPROJ_TILE = 1024
INPROJ_SUBTILES = 4


def _rms(x, gain):
    return x * lax.rsqrt(jnp.mean(x * x, axis=-1, keepdims=True) + EPS) * gain


def _silu(x):
    h = 0.5 * x
    return h + h * jnp.tanh(h)


def _gelu_tanh(x):
    c = 0.7978845608028654
    h = 0.5 * x
    return h + h * jnp.tanh(x * (c + (c * 0.044715) * (x * x)))


def _bdot(a, b):
    return jnp.dot(a.astype(BF16), b.astype(BF16), preferred_element_type=F32)


def _conv_silu_norm(pre, prev_row, next_row, cw_ref, qkv_ref, kt_ref, row0):
    n, width = pre.shape
    sub = F32_SUBLANES
    first = lax.broadcasted_iota(jnp.int32, (sub, width), 0) == 0
    last = lax.broadcasted_iota(jnp.int32, (sub, width), 0) == sub - 1
    x_prev = pltpu.roll(pre, 1, 0)
    x_prev = jnp.concatenate([jnp.where(first, prev_row, x_prev[:sub]), x_prev[sub:]], axis=0)
    x_next = pltpu.roll(pre, n - 1, 0)
    x_next = jnp.concatenate([x_next[:n - sub], jnp.where(last, next_row, x_next[n - sub:])], axis=0)
    qkv = _silu(cw_ref[0:1, :] * x_prev + cw_ref[1:2, :] * pre + cw_ref[2:3, :] * x_next)
    rows = slice(row0, row0 + n)
    for h in range(2 * DN_HEADS):
        cols = slice(h * DN_HEAD_DIM, (h + 1) * DN_HEAD_DIM)
        t = qkv[:, cols]
        scale = DN_HEAD_DIM ** -0.5 if h < DN_HEADS else 1.0
        t = t * (lax.rsqrt(jnp.sum(t * t, axis=-1, keepdims=True) + EPS) * scale)
        qkv_ref[rows, cols] = t.astype(BF16)
        if h >= DN_HEADS:
            kcols = slice((h - DN_HEADS) * DN_HEAD_DIM, (h - DN_HEADS + 1) * DN_HEAD_DIM)
            for c in range(n // DN_CHUNK):
                kt_ref[(row0 + c * DN_CHUNK) // DN_CHUNK, kcols, :] = (
                    t[c * DN_CHUNK:(c + 1) * DN_CHUNK, :].T.astype(BF16))
    qkv_ref[rows, 2 * DN_WIDTH:] = qkv[:, 2 * DN_WIDTH:].astype(BF16)


def _spatial_gating(ps, lng_ref, lnb_ref, ws_ref, bst_ref, og_ref, yb_ref, row0):
    ps = _gelu_tanh(ps)
    u = ps[:, :SG_WIDTH]
    v = ps[:, SG_WIDTH:]
    vc = v - jnp.mean(v, axis=-1, keepdims=True)
    v = vc * lax.rsqrt(jnp.mean(vc * vc, axis=-1, keepdims=True) + EPS) * lng_ref[...] + lnb_ref[...]
    vb = v.astype(BF16)
    for n in range(ps.shape[0] // SG_CHUNK):
        rows = slice(n * SG_CHUNK, (n + 1) * SG_CHUNK)
        for g in range(SG_GROUPS):
            cols = slice(g * SG_GROUP_DIM, (g + 1) * SG_GROUP_DIM)
            mixed = jnp.dot(ws_ref[g], vb[rows, cols], preferred_element_type=F32) + bst_ref[:, g:g + 1]
            y = _rms(u[rows, cols] * mixed, og_ref[:, cols])
            yb_ref[row0 + n * SG_CHUNK:row0 + (n + 1) * SG_CHUNK, cols] = y.astype(BF16)


N_INPROJ_INPUTS = 13
N_INPROJ_OUTPUTS = 5


def _inproj_kernel(*refs, tiles_per_seq, n_casts):
    (x_ref, xprev_ref, xnext_ref, ng_ref, wqkvz_ref, wab_ref, wsg_ref, lng_ref, lnb_ref, ws_ref, bst_ref,
     og_ref, cw_ref) = refs[:N_INPROJ_INPUTS]
    cast_src_refs = refs[N_INPROJ_INPUTS:N_INPROJ_INPUTS + n_casts]
    outs = refs[N_INPROJ_INPUTS + n_casts:]
    qkv_ref, kt_ref, z_ref, gates_ref, yb_ref = outs[:N_INPROJ_OUTPUTS]
    for src_ref, dst_ref in zip(cast_src_refs, outs[N_INPROJ_OUTPUTS:]):
        dst_ref[...] = src_ref[...].astype(dst_ref.dtype)
    tm = x_ref.shape[0]
    st = tm // INPROJ_SUBTILES
    halo = xprev_ref.shape[0]
    sub_rows = [slice(s * st, (s + 1) * st) for s in range(INPROJ_SUBTILES)]
    h_halo = _rms(jnp.concatenate([xprev_ref[...], xnext_ref[...]], axis=0), ng_ref[...]).astype(BF16)
    hs = [_rms(x_ref[r, :], ng_ref[...]).astype(BF16) for r in sub_rows]
    pq0 = jnp.dot(jnp.concatenate([hs[0], h_halo], axis=0), wqkvz_ref[...], preferred_element_type=F32)
    pqs = [pq0[:st]] + [jnp.dot(h, wqkvz_ref[...], preferred_element_type=F32) for h in hs[1:]]
    pss = [jnp.dot(h, wsg_ref[...], preferred_element_type=F32) for h in hs]
    gates_ref[...] = jnp.dot(jnp.concatenate(hs, axis=0), wab_ref[...],
                             preferred_element_type=F32)

    pos = pl.program_id(0) % tiles_per_seq
    tile_prev = pq0[st + halo - 1:st + halo, :OFF_Z] * (pos > 0).astype(F32)
    tile_next = pq0[st + halo:st + halo + 1, :OFF_Z] * (pos < tiles_per_seq - 1).astype(F32)
    for s, r in enumerate(sub_rows):
        z_ref[r, :] = pqs[s][:, OFF_Z:].astype(BF16)
        prev_row = tile_prev if s == 0 else pqs[s - 1][st - 1:st, :OFF_Z]
        next_row = tile_next if s == INPROJ_SUBTILES - 1 else pqs[s + 1][0:1, :OFF_Z]
        _conv_silu_norm(pqs[s][:, :OFF_Z], prev_row, next_row, cw_ref, qkv_ref, kt_ref, s * st)
    for s in range(INPROJ_SUBTILES):
        _spatial_gating(pss[s], lng_ref, lnb_ref, ws_ref, bst_ref, og_ref, yb_ref, s * st)


def _inproj(x2, ng, w_in, wsg, lng, lnb, ws, bst, og, conv_w, seq, layer, casts=()):
    t, d = x2.shape
    tm = INPROJ_TILE
    steps = t // tm
    slab = lambda a: pl.BlockSpec((a.shape[0] // steps, a.shape[1]), lambda i: (i, 0))
    assert all(a.shape[0] % (steps * 2 * F32_SUBLANES) == 0 for a in casts)
    halo = F32_SUBLANES
    per_tile = tm // halo
    n_halo = t // halo
    once = pl.Buffered(1)
    full = lambda *shape: pl.BlockSpec(shape, lambda i: (0,) * len(shape), pipeline_mode=once)
    layer_cols = lambda width, col_block: pl.BlockSpec((None, d, width), lambda i: (layer, 0, col_block),
                                                       pipeline_mode=once)
    rows = lambda w: pl.BlockSpec((tm, w), lambda i: (i, 0))
    prev_spec = pl.BlockSpec((halo, d), lambda i: (jnp.maximum(i * per_tile - 1, 0), 0))
    next_spec = pl.BlockSpec((halo, d), lambda i: (jnp.minimum((i + 1) * per_tile, n_halo - 1), 0))
    return pl.pallas_call(
        functools.partial(_inproj_kernel, tiles_per_seq=seq // tm, n_casts=len(casts)),
        grid=(steps,),
        in_specs=[rows(d), prev_spec, next_spec, full(1, d), layer_cols(OFF_A, 0),
                  layer_cols(LANES, OFF_A // LANES), layer_cols(2 * SG_WIDTH, 0),
                  full(1, SG_WIDTH), full(1, SG_WIDTH),
                  pl.BlockSpec((None, SG_GROUPS, SG_CHUNK, SG_CHUNK), lambda i: (layer, 0, 0, 0),
                               pipeline_mode=once),
                  full(SG_CHUNK, SG_GROUPS), full(1, SG_WIDTH), full(3, OFF_Z)] + [slab(a) for a in casts],
        out_specs=[rows(OFF_Z), pl.BlockSpec((tm // DN_CHUNK, DN_WIDTH, DN_CHUNK), lambda i: (i, 0, 0)),
                   rows(DN_WIDTH), rows(LANES), rows(SG_WIDTH)] + [slab(a) for a in casts],
        out_shape=[jax.ShapeDtypeStruct((t, OFF_Z), BF16),
                   jax.ShapeDtypeStruct((t // DN_CHUNK, DN_WIDTH, DN_CHUNK), BF16),
                   jax.ShapeDtypeStruct((t, DN_WIDTH), BF16),
                   jax.ShapeDtypeStruct((t, LANES), F32), jax.ShapeDtypeStruct((t, SG_WIDTH), BF16)]
        + [jax.ShapeDtypeStruct(a.shape, BF16) for a in casts],
        compiler_params=pltpu.CompilerParams(dimension_semantics=("arbitrary",),
                                             vmem_limit_bytes=VMEM_LIMIT_BYTES),
        name="inproj_sg",
    )(x2, x2, x2, ng, w_in, w_in, wsg, lng, lnb, ws, bst, og, conv_w, *casts)


def _blockdiag(p):
    n = p.shape[0]
    zero = jnp.zeros((n, n), p.dtype)
    return jnp.concatenate([jnp.concatenate([p[:, :n], zero], axis=1),
                            jnp.concatenate([zero, p[:, n:]], axis=1)], axis=0)


def _pdot(p, q):
    return jnp.dot(p.astype(BF16), _blockdiag(q.astype(BF16)), preferred_element_type=F32)


def _unit_tri_inverse(a2s):
    n = a2s[0].shape[0]
    row = lax.broadcasted_iota(jnp.int32, (n, 2 * n), 0)
    colm = lax.broadcasted_iota(jnp.int32, (n, 2 * n), 1) % n
    eye = (row == colm).astype(F32)
    base = NEUMANN_BLOCK
    base_mask = (row // base) == (colm // base)
    ld = [jnp.where(base_mask, a2, 0.0) for a2 in a2s]
    ld2 = [_pdot(x, x) for x in ld]
    t = [eye - x for x in ld]
    both = [_pdot(jnp.concatenate([x, ti], axis=0), x) for x, ti in zip(ld2, t)]
    ld4 = [bo[:n] for bo in both]
    t = [ti + bo[n:] for ti, bo in zip(t, both)]
    t = [ti + _pdot(ti, x) for ti, x in zip(t, ld4)]

    levels = []
    m = base
    while m < n:
        levels.append(((row // (2 * m)) == (colm // (2 * m))) & ((row // m) != (colm // m)))
        m *= 2
    if not levels:
        return t
    a_t = [_pdot(jnp.concatenate([jnp.where(joins, a2, 0.0) for joins in levels], axis=0), ti)
           for a2, ti in zip(a2s, t)]
    w = [[at[j * n:(j + 1) * n] for j in range(len(levels))] for at in a_t]
    for lvl in range(len(levels)):
        x = [wi[lvl] for wi in w]
        prod = [_pdot(jnp.concatenate([ti] + wi[lvl + 1:], axis=0), xi) for ti, wi, xi in zip(t, w, x)]
        t = [ti - pr[:n] for ti, pr in zip(t, prod)]
        w = [wi[:lvl + 1] + [wj - pr[(j + 1) * n:(j + 2) * n] for j, wj in enumerate(wi[lvl + 1:])]
             for wi, pr in zip(w, prod)]
    return t


def _chunk_cumsum(x, rows, reverse):
    n = x.shape[0]
    s = 1
    while s < n:
        if reverse:
            x = x + jnp.where(rows < n - s, pltpu.roll(x, n - s, 0), 0.0)
        else:
            x = x + jnp.where(rows >= s, pltpu.roll(x, s, 0), 0.0)
        s *= 2
    return x


def _delta_operands(reverse, qkv, kt_all, gates, alog_ref, dtb_ref):
    c = DN_CHUNK
    row = lax.broadcasted_iota(jnp.int32, (c, c), 0)
    col = lax.broadcasted_iota(jnp.int32, (c, c), 1)
    mxu_dtype = qkv.dtype

    z = gates + dtb_ref[...]
    softplus = jnp.maximum(z, 0.0) + jnp.log(1.0 + jnp.exp(-jnp.abs(z)))
    g = -jnp.exp(alog_ref[...]) * softplus
    beta = jax.nn.sigmoid(gates)
    gc = _chunk_cumsum(g, row, reverse)
    gct = gc.T
    g_last = gc[0:1, :] if reverse else gc[c - 1:c, :]
    g_last_t = gct[:, 0:1] if reverse else gct[:, c - 1:c]
    e_gc = jnp.exp(gc)
    e_tail_t = jnp.exp(g_last_t - gct).astype(mxu_dtype)
    e_last = jnp.exp(g_last)

    incl = (row <= col) if reverse else (row >= col)
    strict = (row < col) if reverse else (row > col)
    d = 1 if reverse else 0

    chains = []
    for h in range(DN_HEADS):
        r = d * DN_HEADS + h
        q = qkv[:, h * DN_HEAD_DIM:(h + 1) * DN_HEAD_DIM]
        k = qkv[:, DN_WIDTH + h * DN_HEAD_DIM:DN_WIDTH + (h + 1) * DN_HEAD_DIM]
        v = qkv[:, 2 * DN_WIDTH + h * DN_HEAD_DIM:2 * DN_WIDTH + (h + 1) * DN_HEAD_DIM]
        kt = kt_all[h * DN_HEAD_DIM:(h + 1) * DN_HEAD_DIM, :]
        b_bc = jnp.broadcast_to(beta[:, 2 * DN_HEADS + r:2 * DN_HEADS + r + 1],
                                (c, DN_HEAD_DIM)).astype(mxu_dtype)
        eg_bc = jnp.broadcast_to(e_gc[:, r:r + 1], (c, DN_HEAD_DIM)).astype(mxu_dtype)
        kb = k * b_bc
        chains.append(dict(
            reverse=reverse, strict=strict, kt=kt,
            gram_lhs=jnp.concatenate([kb, q], axis=0),
            decay=jnp.exp(jnp.where(incl, gc[:, r:r + 1] - gct[r:r + 1, :], -1e30)),
            rhs=jnp.concatenate([v * b_bc, kb * eg_bc], axis=1),
            qdec=q * eg_bc, ktail=kt * e_tail_t[r:r + 1, :],
            e_last=jnp.broadcast_to(e_last[:, r:r + 1], (1, DN_HEAD_DIM))))
    return chains


def _wy_solve(a_list, rhs_list, rev_list):
    c = DN_CHUNK
    hc = c // 2
    lane = lax.broadcasted_iota(jnp.int32, (hc, c), 1)
    diag = [jnp.where(lane < hc, a[:hc], a[hc:]) for a in a_list]
    t12 = _unit_tri_inverse(diag)
    t21 = [pltpu.roll(t, hc, 1) for t in t12]
    t_first = [(t21 if rev else t12)[i][:, :hc] for i, rev in enumerate(rev_list)]
    t_second = [(t12 if rev else t21)[i][:, :hc] for i, rev in enumerate(rev_list)]
    a_off = [pltpu.roll(a[:hc], hc, 1)[:, :hc] if rev else a[hc:, :hc] for a, rev in zip(a_list, rev_list)]
    r_first = [r_[hc:] if rev else r_[:hc] for r_, rev in zip(rhs_list, rev_list)]
    r_second = [r_[:hc] if rev else r_[hc:] for r_, rev in zip(rhs_list, rev_list)]
    x_first = [_bdot(t, r_) for t, r_ in zip(t_first, r_first)]
    y = [_bdot(a, x) for a, x in zip(a_off, x_first)]
    x_second = [_bdot(t, r_ - y_) for t, r_, y_ in zip(t_second, r_second, y)]
    return [jnp.concatenate([x2, x1] if rev else [x1, x2], axis=0)
            for x1, x2, rev in zip(x_first, x_second, rev_list)]


def _delta_independent(chains):
    c = DN_CHUNK
    grams = [_bdot(p["gram_lhs"], p["kt"]) for p in chains]
    a_s = [jnp.where(p["strict"], gm[:c] * p["decay"], 0.0) for p, gm in zip(chains, grams)]
    attns = [gm[c:] * p["decay"] for p, gm in zip(chains, grams)]
    uw = _wy_solve(a_s, [p["rhs"] for p in chains], [p["reverse"] for p in chains])
    return [dict(u=x[:, :DN_HEAD_DIM],
                 wq=jnp.concatenate([x[:, DN_HEAD_DIM:].astype(BF16), p["qdec"].astype(BF16)], axis=0),
                 ak=jnp.concatenate([at.astype(BF16), p["ktail"].astype(BF16)], axis=0),
                 e_last=p["e_last"])
            for p, x, at in zip(chains, uw, attns)]


def _delta_recurrent(parts, states):
    c = DN_CHUNK
    rss = [_bdot(p["wq"], s_) for p, s_ in zip(parts, states)]
    v_news = [p["u"] - rs[:c] for p, rs in zip(parts, rss)]
    rvs = [_bdot(p["ak"], vn) for p, vn in zip(parts, v_news)]
    outs = [rs[c:] + rv[:c] for rs, rv in zip(rss, rvs)]
    new_states = [s_ * p["e_last"] + rv[c:] for p, s_, rv in zip(parts, states, rvs)]
    return outs, new_states


def _delta_kernel(xf_ref, xb_ref, ktf_ref, ktb_ref, gf_ref, gb_ref, alog_ref, dtb_ref, of_ref, ob_ref, s_ref):
    n_rows = xf_ref.shape[0]
    c = DN_CHUNK
    step_chunks = xf_ref.shape[1] // c

    @pl.when(pl.program_id(1) == 0)
    def _():
        s_ref[...] = jnp.zeros_like(s_ref)

    chains, places = [], []
    for j in range(step_chunks):
        jb = step_chunks - 1 - j
        for b in range(n_rows):
            chains += _delta_operands(False, xf_ref[b, j * c:(j + 1) * c, :], ktf_ref[b, j],
                                      gf_ref[b, j * c:(j + 1) * c, :], alog_ref, dtb_ref)
            places += [(of_ref, b, j, h) for h in range(DN_HEADS)]
            chains += _delta_operands(True, xb_ref[b, jb * c:(jb + 1) * c, :], ktb_ref[b, jb],
                                      gb_ref[b, jb * c:(jb + 1) * c, :], alog_ref, dtb_ref)
            places += [(ob_ref, b, jb, h) for h in range(DN_HEADS)]
    parts = _delta_independent(chains)
    per_chunk = len(chains) // step_chunks
    states = [s_ref[i] for i in range(per_chunk)]
    for j in range(step_chunks):
        sel = slice(j * per_chunk, (j + 1) * per_chunk)
        outs, states = _delta_recurrent(parts[sel], states)
        for o, (o_ref, b, jj, h) in zip(outs, places[sel]):
            o_ref[b, jj * c:(jj + 1) * c, h * DN_HEAD_DIM:(h + 1) * DN_HEAD_DIM] = o
    for i, s_ in enumerate(states):
        s_ref[i] = s_


def _delta_rule(qkv, kt, gates, alog_row, dtb_row):
    bsz, seq, width = qkv.shape
    c = DN_CHUNK
    nb = DN_BATCH_ROWS
    sc = DN_STEP_CHUNKS
    n_steps = seq // (sc * c)
    fwd = lambda n: n
    bwd = lambda n: n_steps - 1 - n
    chunk_spec = lambda w, block_of: pl.BlockSpec((nb, sc * c, w), lambda b, n: (b, block_of(n), 0))
    kt_spec = lambda block_of: pl.BlockSpec((nb, sc, DN_WIDTH, c), lambda b, n: (b, block_of(n), 0, 0))
    full = lambda *shape: pl.BlockSpec(shape, lambda b, n: (0,) * len(shape))
    out_sds = jax.ShapeDtypeStruct((bsz, seq, DN_WIDTH), F32)
    return pl.pallas_call(
        _delta_kernel,
        grid=(bsz // nb, n_steps),
        in_specs=[chunk_spec(width, fwd), chunk_spec(width, bwd), kt_spec(fwd), kt_spec(bwd),
                  chunk_spec(LANES, fwd), chunk_spec(LANES, bwd), full(1, LANES), full(1, LANES)],
        out_specs=[chunk_spec(DN_WIDTH, fwd), chunk_spec(DN_WIDTH, bwd)],
        out_shape=[out_sds, out_sds],
        scratch_shapes=[pltpu.VMEM((2 * nb * DN_HEADS, DN_HEAD_DIM, DN_HEAD_DIM), F32)],
        compiler_params=pltpu.CompilerParams(dimension_semantics=("arbitrary", "arbitrary"),
                                             vmem_limit_bytes=VMEM_LIMIT_BYTES),
        name="delta_rule",
    )(qkv, qkv, kt, kt, gates, gates, alog_row, dtb_row)


def _mix_ffn_kernel(x_ref, of_ref, ob_ref, z_ref, yb_ref, dng_ref, wo_ref, fg_ref, wg_ref, wu_ref, wd_ref,
                    fin_ref, out_ref, *, final):
    x1 = x_ref[...] + jnp.dot(yb_ref[...], wo_ref[DN_WIDTH:, :], preferred_element_type=F32)
    o = of_ref[...] + ob_ref[...]
    z = z_ref[...].astype(F32)
    heads = []
    for h in range(DN_HEADS):
        cols = slice(h * DN_HEAD_DIM, (h + 1) * DN_HEAD_DIM)
        heads.append((_rms(o[:, cols], dng_ref[...]) * _silu(z[:, cols])).astype(BF16))
    x1 = x1 + jnp.dot(jnp.concatenate(heads, axis=1), wo_ref[:DN_WIDTH, :], preferred_element_type=F32)
    hb = _rms(x1, fg_ref[...]).astype(BF16)
    gate = jnp.dot(hb, wg_ref[...], preferred_element_type=F32)
    up = jnp.dot(hb, wu_ref[...], preferred_element_type=F32)
    hid = (_silu(gate) * up).astype(BF16)
    out = x1 + jnp.dot(hid, wd_ref[...], preferred_element_type=F32)
    if final:
        out = _rms(out, fin_ref[...])
    out_ref[...] = out


def _mix_ffn(x2, o_f, o_b, z, yb, dng, wo, fg, wg, wu, wd, fin, final, layer):
    t, d = x2.shape
    f = wg.shape[2]
    tm = TOKEN_TILE
    once = pl.Buffered(1)
    full = lambda *shape: pl.BlockSpec(shape, lambda i: (0,) * len(shape), pipeline_mode=once)
    of_layer = lambda r, c: pl.BlockSpec((None, r, c), lambda i: (layer, 0, 0), pipeline_mode=once)
    rows = lambda w: pl.BlockSpec((tm, w), lambda i: (i, 0))
    return pl.pallas_call(
        functools.partial(_mix_ffn_kernel, final=final),
        grid=(t // tm,),
        in_specs=[rows(d), rows(DN_WIDTH), rows(DN_WIDTH), rows(DN_WIDTH), rows(SG_WIDTH),
                  full(1, DN_HEAD_DIM), of_layer(DN_WIDTH + SG_WIDTH, d), full(1, d), of_layer(d, f),
                  of_layer(d, f), of_layer(f, d), full(1, d)],
        out_specs=rows(d),
        out_shape=jax.ShapeDtypeStruct((t, d), F32),
        compiler_params=pltpu.CompilerParams(dimension_semantics=("arbitrary",),
                                             vmem_limit_bytes=VMEM_LIMIT_BYTES),
        name="mix_ffn",
    )(x2, o_f, o_b, z, yb, dng, wo, fg, wg, wu, wd, fin)


def _pad_lanes(v):
    flat = v.reshape(1, -1).astype(F32)
    return jnp.pad(flat, ((0, 0), (0, LANES - flat.shape[1])))


def kernel(x, mix_norm_g, w_in, conv_w, dn_a_log, dn_dt_bias, dn_norm_g, sg_ln_g, sg_ln_b, sg_w, sg_b,
           sg_out_g, w_out, ffn_norm_g, w_gate, w_up, w_down, final_norm_g):
    bsz, seq, d = x.shape
    depth = w_in.shape[0]
    assert seq % TOKEN_TILE == 0 and seq % INPROJ_TILE == 0 and seq % DN_CHUNK == 0
    assert (INPROJ_TILE // INPROJ_SUBTILES) % SG_CHUNK == 0
    assert bsz % DN_BATCH_ROWS == 0 and seq % (DN_STEP_CHUNKS * DN_CHUNK) == 0
    x2 = x.reshape(bsz * seq, d)
    row = lambda v: v.reshape(1, -1).astype(F32)
    sg_w_b = sg_w.astype(BF16)
    w_in_b = w_in.astype(BF16)
    wsg_b = w_in_b[:, :, OFF_SG:]
    later = (w_out, w_gate, w_up, w_down)
    for l in range(depth):
        outs = _inproj(x2, row(mix_norm_g[l]), w_in_b, wsg_b, row(sg_ln_g[l]), row(sg_ln_b[l]), sg_w_b,
                       sg_b[l].T.astype(F32), row(sg_out_g[l]), conv_w[l].astype(F32), seq, l,
                       casts=tuple(w.reshape(-1, w.shape[-1]) for w in later) if l == 0 else ())
        qkv, kt, z, gates, yb = outs[:N_INPROJ_OUTPUTS]
        if l == 0:
            w_out_b, w_gate_b, w_up_b, w_down_b = (
                o.reshape(w.shape) for o, w in zip(outs[N_INPROJ_OUTPUTS:], later))
        o_f, o_b = _delta_rule(qkv.reshape(bsz, seq, OFF_Z),
                               kt.reshape(bsz, seq // DN_CHUNK, DN_WIDTH, DN_CHUNK),
                               gates.reshape(bsz, seq, LANES),
                               _pad_lanes(dn_a_log[l]), _pad_lanes(dn_dt_bias[l]))
        x2 = _mix_ffn(x2, o_f.reshape(bsz * seq, DN_WIDTH), o_b.reshape(bsz * seq, DN_WIDTH), z, yb,
                      row(dn_norm_g[l]), w_out_b, row(ffn_norm_g[l]), w_gate_b, w_up_b, w_down_b,
                      row(final_norm_g), final=(l == depth - 1), layer=l)
    return x2.reshape(bsz, seq, d)
```

```python
import functools

import jax
import jax.numpy as jnp
from jax import lax
from jax.experimental import pallas as pl
from jax.experimental.pallas import tpu as pltpu

F32 = jnp.float32
BF16 = jnp.bfloat16

DN_HEADS = 4
DN_HEAD_DIM = 128
DN_WIDTH = DN_HEADS * DN_HEAD_DIM
SG_GROUPS = 4
SG_GROUP_DIM = 128
SG_WIDTH = SG_GROUPS * SG_GROUP_DIM
SG_CHUNK = 128
EPS = 1e-6

OFF_Z = 3 * DN_WIDTH
OFF_A = 4 * DN_WIDTH
OFF_B = OFF_A + 2 * DN_HEADS
OFF_SG = OFF_B + 2 * DN_HEADS

LANES = 128
F32_SUBLANES = 8
VMEM_LIMIT_BYTES = 56 * 1024 * 1024

DN_CHUNK = 128
NEUMANN_BLOCK = 8
DN_BATCH_ROWS = 4
DN_STEP_CHUNKS = 2
TOKEN_TILE = 512
INPROJ_TILE = 1024
INPROJ_SUBTILES = 4


def _rms(x, gain):
    return x * lax.rsqrt(jnp.mean(x * x, axis=-1, keepdims=True) + EPS) * gain


def _silu(x):
    h = 0.5 * x
    return h + h * jnp.tanh(h)


def _gelu_tanh(x):
    c = 0.7978845608028654
    h = 0.5 * x
    return h + h * jnp.tanh(x * (c + (c * 0.044715) * (x * x)))


def _bdot(a, b):
    return jnp.dot(a.astype(BF16), b.astype(BF16), preferred_element_type=F32)


def _conv_silu_norm(pre, prev_row, next_row, cw_ref, qkv_ref, kt_ref, row0):
    n, width = pre.shape
    sub = F32_SUBLANES
    first = lax.broadcasted_iota(jnp.int32, (sub, width), 0) == 0
    last = lax.broadcasted_iota(jnp.int32, (sub, width), 0) == sub - 1
    x_prev = pltpu.roll(pre, 1, 0)
    x_prev = jnp.concatenate([jnp.where(first, prev_row, x_prev[:sub]), x_prev[sub:]], axis=0)
    x_next = pltpu.roll(pre, n - 1, 0)
    x_next = jnp.concatenate([x_next[:n - sub], jnp.where(last, next_row, x_next[n - sub:])], axis=0)
    qkv = _silu(cw_ref[0:1, :] * x_prev + cw_ref[1:2, :] * pre + cw_ref[2:3, :] * x_next)
    rows = slice(row0, row0 + n)
    for h in range(2 * DN_HEADS):
        cols = slice(h * DN_HEAD_DIM, (h + 1) * DN_HEAD_DIM)
        t = qkv[:, cols]
        scale = DN_HEAD_DIM ** -0.5 if h < DN_HEADS else 1.0
        t = t * (lax.rsqrt(jnp.sum(t * t, axis=-1, keepdims=True) + EPS) * scale)
        qkv_ref[rows, cols] = t.astype(BF16)
        if h >= DN_HEADS:
            kcols = slice((h - DN_HEADS) * DN_HEAD_DIM, (h - DN_HEADS + 1) * DN_HEAD_DIM)
            for c in range(n // DN_CHUNK):
                kt_ref[(row0 + c * DN_CHUNK) // DN_CHUNK, kcols, :] = (
                    t[c * DN_CHUNK:(c + 1) * DN_CHUNK, :].T.astype(BF16))
    qkv_ref[rows, 2 * DN_WIDTH:] = qkv[:, 2 * DN_WIDTH:].astype(BF16)


def _spatial_gating(ps, lng_ref, lnb_ref, ws_ref, bst_ref, og_ref, yb_ref, row0):
    ps = _gelu_tanh(ps)
    u = ps[:, :SG_WIDTH]
    v = ps[:, SG_WIDTH:]
    vc = v - jnp.mean(v, axis=-1, keepdims=True)
    v = vc * lax.rsqrt(jnp.mean(vc * vc, axis=-1, keepdims=True) + EPS) * lng_ref[...] + lnb_ref[...]
    vb = v.astype(BF16)
    for n in range(ps.shape[0] // SG_CHUNK):
        rows = slice(n * SG_CHUNK, (n + 1) * SG_CHUNK)
        for g in range(SG_GROUPS):
            cols = slice(g * SG_GROUP_DIM, (g + 1) * SG_GROUP_DIM)
            mixed = jnp.dot(ws_ref[g], vb[rows, cols], preferred_element_type=F32) + bst_ref[:, g:g + 1]
            y = _rms(u[rows, cols] * mixed, og_ref[:, cols])
            yb_ref[row0 + n * SG_CHUNK:row0 + (n + 1) * SG_CHUNK, cols] = y.astype(BF16)


N_INPROJ_INPUTS = 13
N_INPROJ_OUTPUTS = 5


def _inproj_kernel(*refs, tiles_per_seq, n_casts):
    (x_ref, xprev_ref, xnext_ref, ng_ref, wqkvz_ref, wab_ref, wsg_ref, lng_ref, lnb_ref, ws_ref, bst_ref,
     og_ref, cw_ref) = refs[:N_INPROJ_INPUTS]
    cast_src_refs = refs[N_INPROJ_INPUTS:N_INPROJ_INPUTS + n_casts]
    outs = refs[N_INPROJ_INPUTS + n_casts:]
    qkv_ref, kt_ref, z_ref, gates_ref, yb_ref = outs[:N_INPROJ_OUTPUTS]
    for src_ref, dst_ref in zip(cast_src_refs, outs[N_INPROJ_OUTPUTS:]):
        dst_ref[...] = src_ref[...].astype(dst_ref.dtype)
    tm = x_ref.shape[0]
    st = tm // INPROJ_SUBTILES
    halo = xprev_ref.shape[0]
    sub_rows = [slice(s * st, (s + 1) * st) for s in range(INPROJ_SUBTILES)]
    h_halo = _rms(jnp.concatenate([xprev_ref[...], xnext_ref[...]], axis=0), ng_ref[...]).astype(BF16)
    hs = [_rms(x_ref[r, :], ng_ref[...]).astype(BF16) for r in sub_rows]
    pq0 = jnp.dot(jnp.concatenate([hs[0], h_halo], axis=0), wqkvz_ref[...], preferred_element_type=F32)
    pqs = [pq0[:st]] + [jnp.dot(h, wqkvz_ref[...], preferred_element_type=F32) for h in hs[1:]]
    pss = [jnp.dot(h, wsg_ref[...], preferred_element_type=F32) for h in hs]
    gates_ref[...] = jnp.dot(jnp.concatenate(hs, axis=0), wab_ref[...],
                             preferred_element_type=F32)

    pos = pl.program_id(0) % tiles_per_seq
    tile_prev = pq0[st + halo - 1:st + halo, :OFF_Z] * (pos > 0).astype(F32)
    tile_next = pq0[st + halo:st + halo + 1, :OFF_Z] * (pos < tiles_per_seq - 1).astype(F32)
    for s, r in enumerate(sub_rows):
        z_ref[r, :] = pqs[s][:, OFF_Z:].astype(BF16)
        prev_row = tile_prev if s == 0 else pqs[s - 1][st - 1:st, :OFF_Z]
        next_row = tile_next if s == INPROJ_SUBTILES - 1 else pqs[s + 1][0:1, :OFF_Z]
        _conv_silu_norm(pqs[s][:, :OFF_Z], prev_row, next_row, cw_ref, qkv_ref, kt_ref, s * st)
    for s in range(INPROJ_SUBTILES):
        _spatial_gating(pss[s], lng_ref, lnb_ref, ws_ref, bst_ref, og_ref, yb_ref, s * st)


def _inproj(x2, ng, w_in, wsg, lng, lnb, ws, bst, og, conv_w, seq, layer, casts=()):
    t, d = x2.shape
    tm = INPROJ_TILE
    steps = t // tm
    slab = lambda a: pl.BlockSpec((a.shape[0] // steps, a.shape[1]), lambda i: (i, 0))
    assert all(a.shape[0] % (steps * 2 * F32_SUBLANES) == 0 for a in casts)
    halo = F32_SUBLANES
    per_tile = tm // halo
    n_halo = t // halo
    once = pl.Buffered(1)
    layer_cols = lambda width, col_block: pl.BlockSpec((None, d, width), lambda i: (layer, 0, col_block),
                                                       pipeline_mode=once)
    per_layer = lambda *shape: pl.BlockSpec((None,) + shape, lambda i: (layer,) + (0,) * len(shape),
                                            pipeline_mode=once)
    rows = lambda w: pl.BlockSpec((tm, w), lambda i: (i, 0))
    prev_spec = pl.BlockSpec((halo, d), lambda i: (jnp.maximum(i * per_tile - 1, 0), 0))
    next_spec = pl.BlockSpec((halo, d), lambda i: (jnp.minimum((i + 1) * per_tile, n_halo - 1), 0))
    return pl.pallas_call(
        functools.partial(_inproj_kernel, tiles_per_seq=seq // tm, n_casts=len(casts)),
        grid=(steps,),
        in_specs=[rows(d), prev_spec, next_spec, per_layer(1, d), layer_cols(OFF_A, 0),
                  layer_cols(LANES, OFF_A // LANES), layer_cols(2 * SG_WIDTH, 0),
                  per_layer(1, SG_WIDTH), per_layer(1, SG_WIDTH), per_layer(SG_GROUPS, SG_CHUNK, SG_CHUNK),
                  per_layer(SG_CHUNK, SG_GROUPS), per_layer(1, SG_WIDTH), per_layer(3, OFF_Z)]
        + [slab(a) for a in casts],
        out_specs=[rows(OFF_Z), pl.BlockSpec((tm // DN_CHUNK, DN_WIDTH, DN_CHUNK), lambda i: (i, 0, 0)),
                   rows(DN_WIDTH), rows(LANES), rows(SG_WIDTH)] + [slab(a) for a in casts],
        out_shape=[jax.ShapeDtypeStruct((t, OFF_Z), BF16),
                   jax.ShapeDtypeStruct((t // DN_CHUNK, DN_WIDTH, DN_CHUNK), BF16),
                   jax.ShapeDtypeStruct((t, DN_WIDTH), BF16),
                   jax.ShapeDtypeStruct((t, LANES), F32), jax.ShapeDtypeStruct((t, SG_WIDTH), BF16)]
        + [jax.ShapeDtypeStruct(a.shape, BF16) for a in casts],
        compiler_params=pltpu.CompilerParams(dimension_semantics=("arbitrary",),
                                             vmem_limit_bytes=VMEM_LIMIT_BYTES),
        name="inproj_sg",
    )(x2, x2, x2, ng, w_in, w_in, wsg, lng, lnb, ws, bst, og, conv_w, *casts)


def _blockdiag(p):
    n = p.shape[0]
    zero = jnp.zeros((n, n), p.dtype)
    return jnp.concatenate([jnp.concatenate([p[:, :n], zero], axis=1),
                            jnp.concatenate([zero, p[:, n:]], axis=1)], axis=0)


def _pdot(p, q):
    return jnp.dot(p.astype(BF16), _blockdiag(q.astype(BF16)), preferred_element_type=F32)


def _unit_tri_inverse(a2s):
    n = a2s[0].shape[0]
    row = lax.broadcasted_iota(jnp.int32, (n, 2 * n), 0)
    colm = lax.broadcasted_iota(jnp.int32, (n, 2 * n), 1) % n
    eye = (row == colm).astype(F32)
    base = NEUMANN_BLOCK
    base_mask = (row // base) == (colm // base)
    ld = [jnp.where(base_mask, a2, 0.0) for a2 in a2s]
    ld2 = [_pdot(x, x) for x in ld]
    t = [eye - x for x in ld]
    both = [_pdot(jnp.concatenate([x, ti], axis=0), x) for x, ti in zip(ld2, t)]
    ld4 = [bo[:n] for bo in both]
    t = [ti + bo[n:] for ti, bo in zip(t, both)]
    t = [ti + _pdot(ti, x) for ti, x in zip(t, ld4)]

    levels = []
    m = base
    while m < n:
        levels.append(((row // (2 * m)) == (colm // (2 * m))) & ((row // m) != (colm // m)))
        m *= 2
    if not levels:
        return t
    a_t = [_pdot(jnp.concatenate([jnp.where(joins, a2, 0.0) for joins in levels], axis=0), ti)
           for a2, ti in zip(a2s, t)]
    w = [[at[j * n:(j + 1) * n] for j in range(len(levels))] for at in a_t]
    for lvl in range(len(levels)):
        x = [wi[lvl] for wi in w]
        prod = [_pdot(jnp.concatenate([ti] + wi[lvl + 1:], axis=0), xi) for ti, wi, xi in zip(t, w, x)]
        t = [ti - pr[:n] for ti, pr in zip(t, prod)]
        w = [wi[:lvl + 1] + [wj - pr[(j + 1) * n:(j + 2) * n] for j, wj in enumerate(wi[lvl + 1:])]
             for wi, pr in zip(w, prod)]
    return t


def _chunk_cumsum(x, rows, reverse):
    n = x.shape[0]
    s = 1
    while s < n:
        if reverse:
            x = x + jnp.where(rows < n - s, pltpu.roll(x, n - s, 0), 0.0)
        else:
            x = x + jnp.where(rows >= s, pltpu.roll(x, s, 0), 0.0)
        s *= 2
    return x


def _delta_operands(reverse, qkv, kt_all, gates, alog_ref, dtb_ref):
    c = DN_CHUNK
    row = lax.broadcasted_iota(jnp.int32, (c, c), 0)
    col = lax.broadcasted_iota(jnp.int32, (c, c), 1)
    mxu_dtype = qkv.dtype

    z = gates + dtb_ref[...]
    softplus = jnp.maximum(z, 0.0) + jnp.log(1.0 + jnp.exp(-jnp.abs(z)))
    g = -jnp.exp(alog_ref[...]) * softplus
    beta = jax.nn.sigmoid(gates)
    gc = _chunk_cumsum(g, row, reverse)
    gct = gc.T
    g_last = gc[0:1, :] if reverse else gc[c - 1:c, :]
    g_last_t = gct[:, 0:1] if reverse else gct[:, c - 1:c]
    e_gc = jnp.exp(gc)
    e_tail_t = jnp.exp(g_last_t - gct).astype(mxu_dtype)
    e_last = jnp.exp(g_last)

    incl = (row <= col) if reverse else (row >= col)
    strict = (row < col) if reverse else (row > col)
    d = 1 if reverse else 0

    chains = []
    for h in range(DN_HEADS):
        r = d * DN_HEADS + h
        q = qkv[:, h * DN_HEAD_DIM:(h + 1) * DN_HEAD_DIM]
        k = qkv[:, DN_WIDTH + h * DN_HEAD_DIM:DN_WIDTH + (h + 1) * DN_HEAD_DIM]
        v = qkv[:, 2 * DN_WIDTH + h * DN_HEAD_DIM:2 * DN_WIDTH + (h + 1) * DN_HEAD_DIM]
        kt = kt_all[h * DN_HEAD_DIM:(h + 1) * DN_HEAD_DIM, :]
        b_bc = jnp.broadcast_to(beta[:, 2 * DN_HEADS + r:2 * DN_HEADS + r + 1],
                                (c, DN_HEAD_DIM)).astype(mxu_dtype)
        eg_bc = jnp.broadcast_to(e_gc[:, r:r + 1], (c, DN_HEAD_DIM)).astype(mxu_dtype)
        kb = k * b_bc
        chains.append(dict(
            reverse=reverse, strict=strict, kt=kt,
            gram_lhs=jnp.concatenate([kb, q], axis=0),
            decay=jnp.exp(jnp.where(incl, gc[:, r:r + 1] - gct[r:r + 1, :], -1e30)),
            rhs=jnp.concatenate([v * b_bc, kb * eg_bc], axis=1),
            qdec=q * eg_bc, ktail=kt * e_tail_t[r:r + 1, :],
            e_last=jnp.broadcast_to(e_last[:, r:r + 1], (1, DN_HEAD_DIM))))
    return chains


def _wy_solve(a_list, rhs_list, rev_list):
    c = DN_CHUNK
    hc = c // 2
    lane = lax.broadcasted_iota(jnp.int32, (hc, c), 1)
    diag = [jnp.where(lane < hc, a[:hc], a[hc:]) for a in a_list]
    t12 = _unit_tri_inverse(diag)
    t21 = [pltpu.roll(t, hc, 1) for t in t12]
    t_first = [(t21 if rev else t12)[i][:, :hc] for i, rev in enumerate(rev_list)]
    t_second = [(t12 if rev else t21)[i][:, :hc] for i, rev in enumerate(rev_list)]
    a_off = [pltpu.roll(a[:hc], hc, 1)[:, :hc] if rev else a[hc:, :hc] for a, rev in zip(a_list, rev_list)]
    r_first = [r_[hc:] if rev else r_[:hc] for r_, rev in zip(rhs_list, rev_list)]
    r_second = [r_[:hc] if rev else r_[hc:] for r_, rev in zip(rhs_list, rev_list)]
    x_first = [_bdot(t, r_) for t, r_ in zip(t_first, r_first)]
    y = [_bdot(a, x) for a, x in zip(a_off, x_first)]
    x_second = [_bdot(t, r_ - y_) for t, r_, y_ in zip(t_second, r_second, y)]
    return [jnp.concatenate([x2, x1] if rev else [x1, x2], axis=0)
            for x1, x2, rev in zip(x_first, x_second, rev_list)]


def _delta_independent(chains):
    c = DN_CHUNK
    grams = [_bdot(p["gram_lhs"], p["kt"]) for p in chains]
    a_s = [jnp.where(p["strict"], gm[:c] * p["decay"], 0.0) for p, gm in zip(chains, grams)]
    attns = [gm[c:] * p["decay"] for p, gm in zip(chains, grams)]
    uw = _wy_solve(a_s, [p["rhs"] for p in chains], [p["reverse"] for p in chains])
    return [dict(u=x[:, :DN_HEAD_DIM],
                 wq=jnp.concatenate([x[:, DN_HEAD_DIM:].astype(BF16), p["qdec"].astype(BF16)], axis=0),
                 ak=jnp.concatenate([at.astype(BF16), p["ktail"].astype(BF16)], axis=0),
                 e_last=p["e_last"])
            for p, x, at in zip(chains, uw, attns)]


def _delta_recurrent(parts, states):
    c = DN_CHUNK
    rss = [_bdot(p["wq"], s_) for p, s_ in zip(parts, states)]
    v_news = [p["u"] - rs[:c] for p, rs in zip(parts, rss)]
    rvs = [_bdot(p["ak"], vn) for p, vn in zip(parts, v_news)]
    outs = [rs[c:] + rv[:c] for rs, rv in zip(rss, rvs)]
    new_states = [s_ * p["e_last"] + rv[c:] for p, s_, rv in zip(parts, states, rvs)]
    return outs, new_states


def _delta_kernel(xf_ref, xb_ref, ktf_ref, ktb_ref, gf_ref, gb_ref, alog_ref, dtb_ref, of_ref, ob_ref, s_ref):
    n_rows = xf_ref.shape[0]
    c = DN_CHUNK
    step_chunks = xf_ref.shape[1] // c

    @pl.when(pl.program_id(1) == 0)
    def _():
        s_ref[...] = jnp.zeros_like(s_ref)

    chains, places = [], []
    for j in range(step_chunks):
        jb = step_chunks - 1 - j
        for b in range(n_rows):
            chains += _delta_operands(False, xf_ref[b, j * c:(j + 1) * c, :], ktf_ref[b, j],
                                      gf_ref[b, j * c:(j + 1) * c, :], alog_ref, dtb_ref)
            places += [(of_ref, b, j, h) for h in range(DN_HEADS)]
            chains += _delta_operands(True, xb_ref[b, jb * c:(jb + 1) * c, :], ktb_ref[b, jb],
                                      gb_ref[b, jb * c:(jb + 1) * c, :], alog_ref, dtb_ref)
            places += [(ob_ref, b, jb, h) for h in range(DN_HEADS)]
    parts = _delta_independent(chains)
    per_chunk = len(chains) // step_chunks
    states = [s_ref[i] for i in range(per_chunk)]
    for j in range(step_chunks):
        sel = slice(j * per_chunk, (j + 1) * per_chunk)
        outs, states = _delta_recurrent(parts[sel], states)
        for o, (o_ref, b, jj, h) in zip(outs, places[sel]):
            o_ref[b, jj * c:(jj + 1) * c, h * DN_HEAD_DIM:(h + 1) * DN_HEAD_DIM] = o
    for i, s_ in enumerate(states):
        s_ref[i] = s_


def _delta_rule(qkv, kt, gates, alog_rows, dtb_rows, layer):
    bsz, seq, width = qkv.shape
    c = DN_CHUNK
    nb = DN_BATCH_ROWS
    sc = DN_STEP_CHUNKS
    n_steps = seq // (sc * c)
    fwd = lambda n: n
    bwd = lambda n: n_steps - 1 - n
    chunk_spec = lambda w, block_of: pl.BlockSpec((nb, sc * c, w), lambda b, n: (b, block_of(n), 0))
    kt_spec = lambda block_of: pl.BlockSpec((nb, sc, DN_WIDTH, c), lambda b, n: (b, block_of(n), 0, 0))
    gate_params = pl.BlockSpec((None, 1, LANES), lambda b, n: (layer, 0, 0))
    out_sds = jax.ShapeDtypeStruct((bsz, seq, DN_WIDTH), F32)
    return pl.pallas_call(
        _delta_kernel,
        grid=(bsz // nb, n_steps),
        in_specs=[chunk_spec(width, fwd), chunk_spec(width, bwd), kt_spec(fwd), kt_spec(bwd),
                  chunk_spec(LANES, fwd), chunk_spec(LANES, bwd), gate_params, gate_params],
        out_specs=[chunk_spec(DN_WIDTH, fwd), chunk_spec(DN_WIDTH, bwd)],
        out_shape=[out_sds, out_sds],
        scratch_shapes=[pltpu.VMEM((2 * nb * DN_HEADS, DN_HEAD_DIM, DN_HEAD_DIM), F32)],
        compiler_params=pltpu.CompilerParams(dimension_semantics=("arbitrary", "arbitrary"),
                                             vmem_limit_bytes=VMEM_LIMIT_BYTES),
        name="delta_rule",
    )(qkv, qkv, kt, kt, gates, gates, alog_rows, dtb_rows)


def _mix_ffn_kernel(x_ref, of_ref, ob_ref, z_ref, yb_ref, dng_ref, wo_ref, fg_ref, wg_ref, wu_ref, wd_ref,
                    fin_ref, out_ref, *, final):
    x1 = x_ref[...] + jnp.dot(yb_ref[...], wo_ref[DN_WIDTH:, :], preferred_element_type=F32)
    o = of_ref[...] + ob_ref[...]
    z = z_ref[...].astype(F32)
    heads = []
    for h in range(DN_HEADS):
        cols = slice(h * DN_HEAD_DIM, (h + 1) * DN_HEAD_DIM)
        heads.append((_rms(o[:, cols], dng_ref[...]) * _silu(z[:, cols])).astype(BF16))
    x1 = x1 + jnp.dot(jnp.concatenate(heads, axis=1), wo_ref[:DN_WIDTH, :], preferred_element_type=F32)
    hb = _rms(x1, fg_ref[...]).astype(BF16)
    gate = jnp.dot(hb, wg_ref[...], preferred_element_type=F32)
    up = jnp.dot(hb, wu_ref[...], preferred_element_type=F32)
    hid = (_silu(gate) * up).astype(BF16)
    out = x1 + jnp.dot(hid, wd_ref[...], preferred_element_type=F32)
    if final:
        out = _rms(out, fin_ref[...])
    out_ref[...] = out


def _mix_ffn(x2, o_f, o_b, z, yb, dng, wo, fg, wg, wu, wd, fin, final, layer):
    t, d = x2.shape
    f = wg.shape[2]
    tm = TOKEN_TILE
    once = pl.Buffered(1)
    full = lambda *shape: pl.BlockSpec(shape, lambda i: (0,) * len(shape), pipeline_mode=once)
    of_layer = lambda r, c: pl.BlockSpec((None, r, c), lambda i: (layer, 0, 0), pipeline_mode=once)
    rows = lambda w: pl.BlockSpec((tm, w), lambda i: (i, 0))
    return pl.pallas_call(
        functools.partial(_mix_ffn_kernel, final=final),
        grid=(t // tm,),
        in_specs=[rows(d), rows(DN_WIDTH), rows(DN_WIDTH), rows(DN_WIDTH), rows(SG_WIDTH),
                  of_layer(1, DN_HEAD_DIM), of_layer(DN_WIDTH + SG_WIDTH, d), of_layer(1, d), of_layer(d, f),
                  of_layer(d, f), of_layer(f, d), full(1, d)],
        out_specs=rows(d),
        out_shape=jax.ShapeDtypeStruct((t, d), F32),
        compiler_params=pltpu.CompilerParams(dimension_semantics=("arbitrary",),
                                             vmem_limit_bytes=VMEM_LIMIT_BYTES),
        name="mix_ffn",
    )(x2, o_f, o_b, z, yb, dng, wo, fg, wg, wu, wd, fin)


def kernel(x, mix_norm_g, w_in, conv_w, dn_a_log, dn_dt_bias, dn_norm_g, sg_ln_g, sg_ln_b, sg_w, sg_b,
           sg_out_g, w_out, ffn_norm_g, w_gate, w_up, w_down, final_norm_g):
    bsz, seq, d = x.shape
    depth = w_in.shape[0]
    assert seq % TOKEN_TILE == 0 and seq % INPROJ_TILE == 0 and seq % DN_CHUNK == 0
    assert (INPROJ_TILE // INPROJ_SUBTILES) % SG_CHUNK == 0
    assert bsz % DN_BATCH_ROWS == 0 and seq % (DN_STEP_CHUNKS * DN_CHUNK) == 0
    x2 = x.reshape(bsz * seq, d)
    rows = lambda v: v.reshape(depth, 1, -1).astype(F32)
    lanes = lambda v: jnp.pad(rows(v), ((0, 0), (0, 0), (0, LANES - v[0].size)))
    sg_w_b = sg_w.astype(BF16)
    w_in_b = w_in.astype(BF16)
    wsg_b = w_in_b[:, :, OFF_SG:]
    later = (w_out, w_gate, w_up, w_down)
    for l in range(depth):
        outs = _inproj(x2, rows(mix_norm_g), w_in_b, wsg_b, rows(sg_ln_g), rows(sg_ln_b), sg_w_b,
                       jnp.swapaxes(sg_b, 1, 2).astype(F32), rows(sg_out_g), conv_w.astype(F32), seq, l,
                       casts=tuple(w.reshape(-1, w.shape[-1]) for w in later) if l == 0 else ())
        qkv, kt, z, gates, yb = outs[:N_INPROJ_OUTPUTS]
        if l == 0:
            w_out_b, w_gate_b, w_up_b, w_down_b = (
                o.reshape(w.shape) for o, w in zip(outs[N_INPROJ_OUTPUTS:], later))
        o_f, o_b = _delta_rule(qkv.reshape(bsz, seq, OFF_Z),
                               kt.reshape(bsz, seq // DN_CHUNK, DN_WIDTH, DN_CHUNK),
                               gates.reshape(bsz, seq, LANES), lanes(dn_a_log), lanes(dn_dt_bias), l)
        x2 = _mix_ffn(x2, o_f.reshape(bsz * seq, DN_WIDTH), o_b.reshape(bsz * seq, DN_WIDTH), z, yb,
                      rows(dn_norm_g), w_out_b, rows(ffn_norm_g), w_gate_b, w_up_b, w_down_b,
                      final_norm_g.reshape(1, -1).astype(F32), final=(l == depth - 1), layer=l)
    return x2.reshape(bsz, seq, d)
```
